```python
import math
import jax, jax.numpy as jnp
from jax import lax
import numpy as np

D_MODEL = 1024
BATCH = 16
SEQ = 2048
DEPTH = 1

MLA_HEADS = 8
MLA_NOPE_DIM = 64
MLA_ROPE_DIM = 32
MLA_V_DIM = 64
MLA_Q_RANK = 384
MLA_KV_RANK = 256
ROPE_THETA = 10000.0
DIFF_HEADS = 8
DIFF_HEAD_DIM = D_MODEL // DIFF_HEADS // 2
DIFF_V_DIM = 2 * DIFF_HEAD_DIM
D_FF = 4 * D_MODEL
N_BRANCHES = 2
Q_BLOCK = 128
LN_EPS = 1e-5
RMS_EPS = 1e-6
NEG_INF = -1e30
MAX_POS_OFFSET = 4096
DEEPNORM_ALPHA = (2.0 * DEPTH) ** 0.25
DEEPNORM_BETA = (8.0 * DEPTH) ** -0.25

DIFF_Q_COLS = DIFF_HEADS * 2 * DIFF_HEAD_DIM
DIFF_K_COLS = DIFF_HEADS * 2 * DIFF_HEAD_DIM
DIFF_V_COLS = DIFF_HEADS * DIFF_V_DIM
GATE_COLS = N_BRANCHES * D_MODEL
_O1 = MLA_Q_RANK
_O2 = _O1 + MLA_KV_RANK
_O3 = _O2 + MLA_ROPE_DIM
_O4 = _O3 + DIFF_Q_COLS
_O5 = _O4 + DIFF_K_COLS
_O6 = _O5 + DIFF_V_COLS
D_IN = _O6 + GATE_COLS
SPLIT_POINTS = (_O1, _O2, _O3, _O4, _O5, _O6)

kernel_name = "hybrid_mla_diffattn_alibi_deepnorm"


def layer_norm(x, g, b):
    xf = x.astype(jnp.float32)
    mu = jnp.mean(xf, axis=-1, keepdims=True)
    var = jnp.mean(jnp.square(xf - mu), axis=-1, keepdims=True)
    return ((xf - mu) * lax.rsqrt(var + LN_EPS) * g.astype(jnp.float32) + b.astype(jnp.float32)).astype(x.dtype)


def rms_norm(x, g):
    xf = x.astype(jnp.float32)
    ms = jnp.mean(jnp.square(xf), axis=-1, keepdims=True)
    return (xf * lax.rsqrt(ms + RMS_EPS) * g.astype(jnp.float32)).astype(x.dtype)


def rope_angles(positions):
    inv_freq = 1.0 / (ROPE_THETA ** (jnp.arange(0, MLA_ROPE_DIM, 2, dtype=jnp.float32) / MLA_ROPE_DIM))
    ang = positions.astype(jnp.float32)[..., None] * inv_freq
    return jnp.cos(ang), jnp.sin(ang)


def apply_rope(t, cos, sin):
    t1, t2 = jnp.split(t.astype(jnp.float32), 2, axis=-1)
    return jnp.concatenate([t1 * cos - t2 * sin, t2 * cos + t1 * sin], axis=-1).astype(t.dtype)


def alibi_slopes(n):
    def pow2_slopes(k):
        start = 2.0 ** (-8.0 / k)
        return [start ** (i + 1) for i in range(k)]
    if math.log2(n).is_integer():
        s = pow2_slopes(n)
    else:
        c = 2 ** int(math.floor(math.log2(n)))
        s = pow2_slopes(c) + pow2_slopes(2 * c)[0::2][: n - c]
    return np.asarray(s, dtype=np.float32)


def _to_blocks(t):
    b, s = t.shape[:2]
    t = t.reshape((b, s // Q_BLOCK, Q_BLOCK) + t.shape[2:])
    return jnp.moveaxis(t, 1, 0)


def _from_blocks(t):
    t = jnp.moveaxis(t, 0, 1)
    return t.reshape((t.shape[0], t.shape[1] * t.shape[2]) + t.shape[3:])


def mla_attention(q_nope, q_pe, k_nope, k_pe, v):
    seq = q_nope.shape[1]
    scale = (MLA_NOPE_DIM + MLA_ROPE_DIM) ** -0.5
    k_idx = jnp.arange(seq)

    def one_block(args):
        qn, qp, blk = args
        q_idx = blk * Q_BLOCK + jnp.arange(Q_BLOCK)
        s = (jnp.einsum('bqhd,bkhd->bhqk', qn, k_nope, preferred_element_type=jnp.float32)
             + jnp.einsum('bqhr,bkr->bhqk', qp, k_pe, preferred_element_type=jnp.float32)) * scale
        s = jnp.where(k_idx[None, :] <= q_idx[:, None], s, NEG_INF)
        p = jax.nn.softmax(s, axis=-1).astype(v.dtype)
        return jnp.einsum('bhqk,bkhe->bqhe', p, v)

    out = lax.map(one_block, (_to_blocks(q_nope), _to_blocks(q_pe), jnp.arange(seq // Q_BLOCK)))
    return _from_blocks(out)


def diff_attention(q, k, v, positions, lam, slopes):
    seq = q.shape[1]
    scale = DIFF_HEAD_DIM ** -0.5
    k_idx = jnp.arange(seq)

    def one_block(args):
        qb, pq, blk = args
        q_idx = blk * Q_BLOCK + jnp.arange(Q_BLOCK)
        s = jnp.einsum('bqhcd,bkhcd->bhcqk', qb, k, preferred_element_type=jnp.float32) * scale
        dist = jnp.abs(pq[:, :, None] - positions[:, None, :]).astype(jnp.float32)
        s = s - slopes[None, :, None, None, None] * dist[:, None, None]
        s = jnp.where(k_idx[None, :] <= q_idx[:, None], s, NEG_INF)
        p = jax.nn.softmax(s, axis=-1)
        a = p[:, :, 0] - lam * p[:, :, 1]
        return jnp.einsum('bhqk,bkhe->bqhe', a.astype(v.dtype), v)

    out = lax.map(one_block, (_to_blocks(q), _to_blocks(positions), jnp.arange(seq // Q_BLOCK)))
    return _from_blocks(out)


def setup_inputs(seed: int = 0) -> dict:
    key = jax.random.key(seed)
    ks = jax.random.split(key, 24)
    L = DEPTH
    beta = DEEPNORM_BETA

    def dense(k, fan_in, fan_out, scale=1.0):
        return jax.random.normal(k, (L, fan_in, fan_out), jnp.float32) * (scale * fan_in ** -0.5)

    def gain(k, n):
        return 1.0 + 0.02 * jax.random.normal(k, (L, n), jnp.float32)

    def small(k, n, s=0.02):
        return s * jax.random.normal(k, (L, n), jnp.float32)

    x = jax.random.normal(ks[0], (BATCH, SEQ, D_MODEL), jnp.float32)
    offset = jax.random.randint(ks[1], (BATCH, 1), 0, MAX_POS_OFFSET, dtype=jnp.int32)
    positions = (offset + jnp.arange(SEQ, dtype=jnp.int32)[None, :]).astype(jnp.int32)

    in_scale = np.ones((D_IN,), np.float32)
    in_scale[_O5:_O6] = beta
    w_in = dense(ks[2], D_MODEL, D_IN) * jnp.asarray(in_scale)
    b_gate = small(ks[3], GATE_COLS)

    mla_q_norm = gain(ks[4], MLA_Q_RANK)
    mla_kv_norm = gain(ks[5], MLA_KV_RANK)
    w_uq = dense(ks[6], MLA_Q_RANK, MLA_HEADS * (MLA_NOPE_DIM + MLA_ROPE_DIM))
    ukv_scale = np.tile(np.concatenate([np.ones(MLA_NOPE_DIM, np.float32), np.full(MLA_V_DIM, beta, np.float32)]), MLA_HEADS)
    w_ukv = dense(ks[7], MLA_KV_RANK, MLA_HEADS * (MLA_NOPE_DIM + MLA_V_DIM)) * jnp.asarray(ukv_scale)
    w_o_mla = dense(ks[8], MLA_HEADS * MLA_V_DIM, D_MODEL, beta)

    diff_lambda_q1 = small(ks[9], DIFF_HEAD_DIM, 0.1)
    diff_lambda_k1 = small(ks[10], DIFF_HEAD_DIM, 0.1)
    diff_lambda_q2 = small(ks[11], DIFF_HEAD_DIM, 0.1)
    diff_lambda_k2 = small(ks[12], DIFF_HEAD_DIM, 0.1)
    diff_subln = gain(ks[13], DIFF_V_DIM)
    w_o_diff = dense(ks[14], DIFF_HEADS * DIFF_V_DIM, D_MODEL, beta)

    w_o = dense(ks[15], D_MODEL, D_MODEL, beta)
    ln1_g = gain(ks[16], D_MODEL)
    ln1_b = small(ks[17], D_MODEL)
    w_up = dense(ks[18], D_MODEL, D_FF, beta)
    w_down = dense(ks[19], D_FF, D_MODEL, beta)
    ln2_g = gain(ks[20], D_MODEL)
    ln2_b = small(ks[21], D_MODEL)
    return {"x": x, "positions": positions, "w_in": w_in, "b_gate": b_gate,
            "mla_q_norm": mla_q_norm, "mla_kv_norm": mla_kv_norm, "w_uq": w_uq, "w_ukv": w_ukv, "w_o_mla": w_o_mla,
            "diff_lambda_q1": diff_lambda_q1, "diff_lambda_k1": diff_lambda_k1,
            "diff_lambda_q2": diff_lambda_q2, "diff_lambda_k2": diff_lambda_k2,
            "diff_subln": diff_subln, "w_o_diff": w_o_diff, "w_o": w_o,
            "ln1_g": ln1_g, "ln1_b": ln1_b, "w_up": w_up, "w_down": w_down, "ln2_g": ln2_g, "ln2_b": ln2_b}


def reference(x, positions, w_in, b_gate, mla_q_norm, mla_kv_norm, w_uq, w_ukv, w_o_mla,
              diff_lambda_q1, diff_lambda_k1, diff_lambda_q2, diff_lambda_k2, diff_subln, w_o_diff, w_o,
              ln1_g, ln1_b, w_up, w_down, ln2_g, ln2_b):
    b, s, _ = x.shape
    cos, sin = rope_angles(positions)
    slopes = jnp.asarray(alibi_slopes(DIFF_HEADS))
    for layer in range(DEPTH):
        lambda_init = 0.8 - 0.6 * math.exp(-0.3 * layer)
        proj = x @ w_in[layer]
        c_q, c_kv, k_pe, dq, dk, dv, gate = jnp.split(proj, SPLIT_POINTS, axis=-1)

        c_q = rms_norm(c_q, mla_q_norm[layer])
        q = (c_q @ w_uq[layer]).reshape(b, s, MLA_HEADS, MLA_NOPE_DIM + MLA_ROPE_DIM)
        q_nope, q_pe = q[..., :MLA_NOPE_DIM], q[..., MLA_NOPE_DIM:]
        q_pe = apply_rope(q_pe, cos[:, :, None], sin[:, :, None])
        k_pe = apply_rope(k_pe, cos, sin)
        c_kv = rms_norm(c_kv, mla_kv_norm[layer])
        kv = (c_kv @ w_ukv[layer]).reshape(b, s, MLA_HEADS, MLA_NOPE_DIM + MLA_V_DIM)
        k_nope, v_mla = kv[..., :MLA_NOPE_DIM], kv[..., MLA_NOPE_DIM:]
        o_mla = mla_attention(q_nope, q_pe, k_nope, k_pe, v_mla).reshape(b, s, MLA_HEADS * MLA_V_DIM)
        y_mla = o_mla @ w_o_mla[layer]

        lam = (jnp.exp(jnp.sum(diff_lambda_q1[layer].astype(jnp.float32) * diff_lambda_k1[layer].astype(jnp.float32)))
               - jnp.exp(jnp.sum(diff_lambda_q2[layer].astype(jnp.float32) * diff_lambda_k2[layer].astype(jnp.float32)))
               + lambda_init)
        o_diff = diff_attention(dq.reshape(b, s, DIFF_HEADS, 2, DIFF_HEAD_DIM),
                                dk.reshape(b, s, DIFF_HEADS, 2, DIFF_HEAD_DIM),
                                dv.reshape(b, s, DIFF_HEADS, DIFF_V_DIM),
                                positions, lam, slopes)
        o_diff = rms_norm(o_diff, diff_subln[layer]) * (1.0 - lambda_init)
        y_diff = o_diff.reshape(b, s, DIFF_HEADS * DIFF_V_DIM) @ w_o_diff[layer]

        g = jax.nn.sigmoid(gate + b_gate[layer]).reshape(b, s, N_BRANCHES, D_MODEL)
        mixed = (g[:, :, 0] * y_mla + g[:, :, 1] * y_diff) @ w_o[layer]
        x = layer_norm(DEEPNORM_ALPHA * x + mixed, ln1_g[layer], ln1_b[layer])

        h = jnp.square(jax.nn.relu(x @ w_up[layer]))
        x = layer_norm(DEEPNORM_ALPHA * x + h @ w_down[layer], ln2_g[layer], ln2_b[layer])
    return x
```

```python
import functools
import math

import jax
import jax.numpy as jnp
import numpy as np
from jax import lax
from jax.experimental import pallas as pl
from jax.experimental.pallas import tpu as pltpu

D_MODEL = 1024
MLA_HEADS = 8
MLA_NOPE_DIM = 64
MLA_ROPE_DIM = 32
MLA_V_DIM = 64
MLA_Q_RANK = 384
MLA_KV_RANK = 256
ROPE_THETA = 10000.0
DIFF_HEADS = 8
DIFF_HEAD_DIM = 64
DIFF_V_DIM = 128
D_FF = 4 * D_MODEL
DEPTH = 1
LN_EPS = 1e-5
RMS_EPS = 1e-6
NEG_INF = -1e30
DEEPNORM_ALPHA = (2.0 * DEPTH) ** 0.25
LAMBDA_INIT = 0.8 - 0.6 * math.exp(-0.3 * 0)

LANES = 128
HEAD_SLAB = LANES
VMEM_LIMIT = 56 * 1024 * 1024

PROJ_ROWS = 512
ATTN_Q = 256
ATTN_K = 256
POST_ROWS = 256
FF_CHUNK = 1024

_O1 = MLA_Q_RANK
_O2 = _O1 + MLA_KV_RANK
_O3 = _O2 + MLA_ROPE_DIM
_O4 = _O3 + DIFF_HEADS * 2 * DIFF_HEAD_DIM
_O5 = _O4 + DIFF_HEADS * 2 * DIFF_HEAD_DIM
_O6 = _O5 + DIFF_HEADS * DIFF_V_DIM


def _alibi_slopes(n):
    def pow2_slopes(k):
        start = 2.0 ** (-8.0 / k)
        return [start ** (i + 1) for i in range(k)]
    if math.log2(n).is_integer():
        s = pow2_slopes(n)
    else:
        c = 2 ** int(math.floor(math.log2(n)))
        s = pow2_slopes(c) + pow2_slopes(2 * c)[0::2][: n - c]
    return np.asarray(s, dtype=np.float32)


def _resident(shape):
    return pl.BlockSpec(shape, lambda *_: (0,) * len(shape), pipeline_mode=pl.Buffered(1))


def _rms(x, g):
    ms = jnp.mean(x * x, axis=-1, keepdims=True)
    return x * lax.rsqrt(ms + RMS_EPS) * g


def _layer_norm(x, g, b):
    mu = jnp.mean(x, axis=-1, keepdims=True)
    xc = x - mu
    var = jnp.mean(xc * xc, axis=-1, keepdims=True)
    return xc * lax.rsqrt(var + LN_EPS) * g + b


def _dot(a, b):
    return jnp.dot(a, b, preferred_element_type=jnp.float32)


def _dot_nt(a, b):
    return lax.dot_general(a, b, (((1,), (1,)), ((), ())), preferred_element_type=jnp.float32)


def _proj_kernel(x_ref, pos_ref, invf_ref, wcq_ref, wckv_ref, wkpe_ref, wd_ref, wuq_ref, wukv_ref,
                 gq_ref, gkv_ref, q_ref, k_ref, v_ref, dq_ref, dk_ref, dv_ref):
    xb = x_ref[...].astype(jnp.bfloat16)
    ang = pos_ref[...].astype(jnp.float32) * invf_ref[...]
    cos = jnp.cos(ang)
    sin = jnp.sin(ang)

    cq = _rms(_dot(xb, wcq_ref[...]), gq_ref[...]).astype(jnp.bfloat16)
    qq = _dot(cq, wuq_ref[...])
    scale = (MLA_NOPE_DIM + MLA_ROPE_DIM) ** -0.5
    cos_s = cos * scale
    sin_s = sin * scale
    half = MLA_HEADS * HEAD_SLAB
    for h in range(MLA_HEADS):
        lo = h * HEAD_SLAB
        q_ref[:, lo:lo + HEAD_SLAB] = (qq[:, lo:lo + HEAD_SLAB] * cos_s
                                       + qq[:, half + lo:half + lo + HEAD_SLAB] * sin_s).astype(q_ref.dtype)

    kp = _dot(xb, wkpe_ref[...])
    kpe = kp[:, :HEAD_SLAB] * cos + kp[:, HEAD_SLAB:] * sin

    ckv = _rms(_dot(xb, wckv_ref[...]), gkv_ref[...]).astype(jnp.bfloat16)
    kv = _dot(ckv, wukv_ref[...])
    for h in range(MLA_HEADS):
        lo = h * HEAD_SLAB
        k_ref[:, lo:lo + HEAD_SLAB] = (kv[:, lo:lo + HEAD_SLAB] + kpe).astype(k_ref.dtype)
    v_ref[...] = kv[:, half:].astype(v_ref.dtype)

    d = _dot(xb, wd_ref[...])
    n = DIFF_HEADS * 2 * DIFF_HEAD_DIM
    dq_ref[...] = (d[:, :n] * (DIFF_HEAD_DIM ** -0.5)).astype(dq_ref.dtype)
    dk_ref[...] = d[:, n:2 * n].astype(dk_ref.dtype)
    dv_ref[...] = d[:, 2 * n:].astype(dv_ref.dtype)


def _proj(x2, pos_col, invf, wcq, wckv, wkpe, wd, wuq, wukv, gq, gkv):
    t = x2.shape[0]
    rows = PROJ_ROWS
    bf = jnp.bfloat16
    tok = lambda w: pl.BlockSpec((rows, w), lambda i: (i, 0))
    out_shape = [jax.ShapeDtypeStruct((t, MLA_HEADS * HEAD_SLAB), bf),
                 jax.ShapeDtypeStruct((t, MLA_HEADS * HEAD_SLAB), bf),
                 jax.ShapeDtypeStruct((t, MLA_HEADS * MLA_V_DIM), bf),
                 jax.ShapeDtypeStruct((t, D_MODEL), bf),
                 jax.ShapeDtypeStruct((t, D_MODEL), bf),
                 jax.ShapeDtypeStruct((t, D_MODEL), bf)]
    return pl.pallas_call(
        _proj_kernel,
        grid=(t // rows,),
        in_specs=[tok(D_MODEL), tok(1), _resident(invf.shape), _resident(wcq.shape), _resident(wckv.shape),
                  _resident(wkpe.shape), _resident(wd.shape), _resident(wuq.shape), _resident(wukv.shape),
                  _resident(gq.shape), _resident(gkv.shape)],
        out_specs=[tok(s.shape[1]) for s in out_shape],
        out_shape=out_shape,
        compiler_params=pltpu.CompilerParams(dimension_semantics=("parallel",), vmem_limit_bytes=VMEM_LIMIT),
        name="proj",
    )(x2, pos_col, invf, wcq, wckv, wkpe, wd, wuq, wukv, gq, gkv)


def _causal_mask(tq, tk):
    row = lax.broadcasted_iota(jnp.int32, (tq, tk), 0)
    col = lax.broadcasted_iota(jnp.int32, (tq, tk), 1)
    return col <= row


def _online_step(s, v, carry):
    m, l, acc = carry
    m_new = jnp.maximum(m, jnp.max(s, axis=-1, keepdims=True))
    alpha = jnp.exp(m - m_new)
    p = jnp.exp(s - m_new)
    l = alpha * l + jnp.sum(p, axis=-1, keepdims=True)
    acc = alpha * acc + _dot(p.astype(jnp.bfloat16), v)
    return m_new, l, acc


def _flash_init(tq, dv):
    return (jnp.full((tq, 1), NEG_INF, jnp.float32), jnp.zeros((tq, 1), jnp.float32),
            jnp.zeros((tq, dv), jnp.float32))


def _mla_kernel(q_ref, k_ref, v_ref, o_ref):
    i = pl.program_id(2)
    tq, tk = ATTN_Q, ATTN_K
    mask = _causal_mask(tq, tk)
    outs = []
    for hh in range(2):
        lo = hh * HEAD_SLAB
        q = q_ref[0, :, lo:lo + HEAD_SLAB]

        def scores(j):
            start = pl.multiple_of(j * tk, tk)
            k = k_ref[0, pl.ds(start, tk), lo:lo + HEAD_SLAB]
            return _dot_nt(q, k), v_ref[0, pl.ds(start, tk), :]

        def body(j, carry):
            s, v = scores(j)
            return _online_step(s, v, carry)

        carry = lax.fori_loop(0, i, body, _flash_init(tq, 2 * MLA_V_DIM))
        s, v = scores(i)
        _, l, acc = _online_step(jnp.where(mask, s, NEG_INF), v, carry)
        outs.append(acc / l)
    lane = lax.broadcasted_iota(jnp.int32, (tq, 2 * MLA_V_DIM), 1)
    o_ref[0] = jnp.where(lane < MLA_V_DIM, outs[0], outs[1]).astype(o_ref.dtype)


def _mla_attention(q, k, v):
    b, s, _ = q.shape
    tq = ATTN_Q
    pairs = MLA_HEADS // 2
    return pl.pallas_call(
        _mla_kernel,
        grid=(b, pairs, s // tq),
        in_specs=[pl.BlockSpec((1, tq, 2 * HEAD_SLAB), lambda bi, p, i: (bi, i, p)),
                  pl.BlockSpec((1, s, 2 * HEAD_SLAB), lambda bi, p, i: (bi, 0, p)),
                  pl.BlockSpec((1, s, 2 * MLA_V_DIM), lambda bi, p, i: (bi, 0, p))],
        out_specs=pl.BlockSpec((1, tq, 2 * MLA_V_DIM), lambda bi, p, i: (bi, i, p)),
        out_shape=jax.ShapeDtypeStruct((b, s, MLA_HEADS * MLA_V_DIM), jnp.bfloat16),
        compiler_params=pltpu.CompilerParams(dimension_semantics=("parallel", "parallel", "arbitrary"),
                                             vmem_limit_bytes=VMEM_LIMIT),
        name="mla_attn",
    )(q, k, v)


def _diff_kernel(slopes_ref, q_ref, k_ref, v_ref, pq_ref, pk_ref, lq1_ref, lk1_ref, lq2_ref, lk2_ref, g_ref,
                 o_ref):
    h = pl.program_id(1)
    i = pl.program_id(2)
    tq, tk = ATTN_Q, ATTN_K
    mask = _causal_mask(tq, tk)
    slope = slopes_ref[h]
    q = q_ref[0]
    lane = lax.broadcasted_iota(jnp.int32, q.shape, 1)
    zero = jnp.zeros_like(q)
    q0 = jnp.where(lane < DIFF_HEAD_DIM, q, zero)
    q1 = jnp.where(lane < DIFF_HEAD_DIM, zero, q)
    pq = pq_ref[0].astype(jnp.float32) * slope

    def scores(j):
        start = pl.multiple_of(j * tk, tk)
        k = k_ref[0, pl.ds(start, tk), :]
        v = v_ref[0, pl.ds(start, tk), :]
        pk = pk_ref[0, j].astype(jnp.float32) * slope
        bias = jnp.abs(pq - pk)
        return _dot_nt(q0, k) - bias, _dot_nt(q1, k) - bias, v

    def body(j, carry):
        s0, s1, v = scores(j)
        return _online_step(s0, v, carry[0]), _online_step(s1, v, carry[1])

    init = _flash_init(tq, DIFF_V_DIM)
    c0, c1 = lax.fori_loop(0, i, body, (init, init))
    s0, s1, v = scores(i)
    _, l0, a0 = _online_step(jnp.where(mask, s0, NEG_INF), v, c0)
    _, l1, a1 = _online_step(jnp.where(mask, s1, NEG_INF), v, c1)

    lam = (jnp.exp(jnp.sum(lq1_ref[...] * lk1_ref[...], axis=-1, keepdims=True))
           - jnp.exp(jnp.sum(lq2_ref[...] * lk2_ref[...], axis=-1, keepdims=True)) + LAMBDA_INIT)
    o = a0 / l0 - lam * (a1 / l1)
    o_ref[0] = (_rms(o, g_ref[...]) * (1.0 - LAMBDA_INIT)).astype(o_ref.dtype)


def _diff_attention(slopes, q, k, v, pos_col, pos_row, lq1, lk1, lq2, lk2, g):
    b, s, _ = q.shape
    tq, tk = ATTN_Q, ATTN_K
    vec = lambda a: pl.BlockSpec(a.shape, lambda bi, h, i: (0, 0))
    return pl.pallas_call(
        _diff_kernel,
        grid=(b, DIFF_HEADS, s // tq),
        in_specs=[pl.BlockSpec(memory_space=pltpu.SMEM),
                  pl.BlockSpec((1, tq, LANES), lambda bi, h, i: (bi, i, h)),
                  pl.BlockSpec((1, s, LANES), lambda bi, h, i: (bi, 0, h)),
                  pl.BlockSpec((1, s, LANES), lambda bi, h, i: (bi, 0, h)),
                  pl.BlockSpec((1, tq, 1), lambda bi, h, i: (bi, i, 0)),
                  pl.BlockSpec((1, s // tk, 1, tk), lambda bi, h, i: (bi, 0, 0, 0)),
                  vec(lq1), vec(lk1), vec(lq2), vec(lk2), vec(g)],
        out_specs=pl.BlockSpec((1, tq, LANES), lambda bi, h, i: (bi, i, h)),
        out_shape=jax.ShapeDtypeStruct((b, s, DIFF_HEADS * DIFF_V_DIM), jnp.bfloat16),
        compiler_params=pltpu.CompilerParams(dimension_semantics=("parallel", "parallel", "arbitrary"),
                                             vmem_limit_bytes=VMEM_LIMIT),
        name="diff_attn",
    )(slopes, q, k, v, pos_col, pos_row, lq1, lk1, lq2, lk2, g)


def _post_kernel(x_ref, om_ref, od_ref, wg_ref, bg_ref, wom_ref, wod_ref, wo_ref, g1_ref, b1_ref,
                 wup_ref, wdn_ref, g2_ref, b2_ref, o_ref):
    x = x_ref[...]
    xb = x.astype(jnp.bfloat16)
    gate = jax.nn.sigmoid(_dot(xb, wg_ref[...]) + bg_ref[...])
    y_mla = _dot(om_ref[...], wom_ref[...])
    y_diff = _dot(od_ref[...], wod_ref[...])
    merged = gate[:, :D_MODEL] * y_mla + gate[:, D_MODEL:] * y_diff
    mixed = _dot(merged.astype(jnp.bfloat16), wo_ref[...])
    x1 = _layer_norm(DEEPNORM_ALPHA * x + mixed, g1_ref[...], b1_ref[...])
    x1b = x1.astype(jnp.bfloat16)
    acc = DEEPNORM_ALPHA * x1
    for c in range(D_FF // FF_CHUNK):
        lo = c * FF_CHUNK
        hid = jnp.maximum(_dot(x1b, wup_ref[:, lo:lo + FF_CHUNK]), 0.0)
        acc = acc + _dot((hid * hid).astype(jnp.bfloat16), wdn_ref[lo:lo + FF_CHUNK, :])
    o_ref[...] = _layer_norm(acc, g2_ref[...], b2_ref[...]).astype(o_ref.dtype)


def _post(x2, o_mla, o_diff, wg, bg, wom, wod, wo, g1, b1, wup, wdn, g2, b2):
    t = x2.shape[0]
    rows = POST_ROWS
    tok = lambda w: pl.BlockSpec((rows, w), lambda i: (i, 0))
    consts = (wg, bg, wom, wod, wo, g1, b1, wup, wdn, g2, b2)
    return pl.pallas_call(
        _post_kernel,
        grid=(t // rows,),
        in_specs=[tok(D_MODEL), tok(o_mla.shape[1]), tok(o_diff.shape[1])] + [_resident(a.shape) for a in consts],
        out_specs=tok(D_MODEL),
        out_shape=jax.ShapeDtypeStruct((t, D_MODEL), jnp.float32),
        compiler_params=pltpu.CompilerParams(dimension_semantics=("parallel",), vmem_limit_bytes=VMEM_LIMIT),
        name="post",
    )(x2, o_mla, o_diff, *consts)


def _rot_half_cols(w):
    half = w.shape[-1] // 2
    return jnp.concatenate([-w[..., half:], w[..., :half]], axis=-1)


def _prep_weights(w_in, w_uq, w_ukv):
    bf = jnp.bfloat16
    f32 = jnp.float32
    wcq = w_in[:, :_O1].astype(bf)
    wckv = w_in[:, _O1:_O2].astype(bf)
    wkpe_raw = w_in[:, _O2:_O3]
    wd = w_in[:, _O3:_O6].astype(bf)
    wg = w_in[:, _O6:].astype(bf)

    pad_r = HEAD_SLAB - MLA_NOPE_DIM - MLA_ROPE_DIM
    d = w_in.shape[0]
    slab = lambda w: jnp.concatenate([jnp.zeros((d, MLA_NOPE_DIM), f32), w, jnp.zeros((d, pad_r), f32)], axis=1)
    wkpe = jnp.concatenate([slab(wkpe_raw), slab(_rot_half_cols(wkpe_raw))], axis=1).astype(bf)

    r = w_uq.shape[0]
    uq = w_uq.reshape(r, MLA_HEADS, MLA_NOPE_DIM + MLA_ROPE_DIM)
    nope, pe = uq[..., :MLA_NOPE_DIM], uq[..., MLA_NOPE_DIM:]
    plain = jnp.concatenate([nope, pe, jnp.zeros((r, MLA_HEADS, pad_r), f32)], axis=-1)
    rot = jnp.concatenate([jnp.zeros_like(nope), _rot_half_cols(pe), jnp.zeros((r, MLA_HEADS, pad_r), f32)], axis=-1)
    wuq = jnp.concatenate([plain.reshape(r, -1), rot.reshape(r, -1)], axis=1).astype(bf)

    r = w_ukv.shape[0]
    ukv = w_ukv.reshape(r, MLA_HEADS, MLA_NOPE_DIM + MLA_V_DIM)
    kn, vv = ukv[..., :MLA_NOPE_DIM], ukv[..., MLA_NOPE_DIM:]
    kslab = jnp.concatenate([kn, jnp.zeros((r, MLA_HEADS, HEAD_SLAB - MLA_NOPE_DIM), f32)], axis=-1)
    wukv = jnp.concatenate([kslab.reshape(r, -1), vv.reshape(r, -1)], axis=1).astype(bf)
    return wcq, wckv, wkpe, wd, wg, wuq, wukv


def kernel(x, positions, w_in, b_gate, mla_q_norm, mla_kv_norm, w_uq, w_ukv, w_o_mla, diff_lambda_q1, diff_lambda_k1, diff_lambda_q2, diff_lambda_k2, diff_subln, w_o_diff, w_o, ln1_g, ln1_b, w_up, w_down, ln2_g, ln2_b):
    b, s, d = x.shape
    bf = jnp.bfloat16
    layer = 0
    wcq, wckv, wkpe, wd, wg, wuq, wukv = _prep_weights(w_in[layer], w_uq[layer], w_ukv[layer])

    inv_freq = 1.0 / (ROPE_THETA ** (jnp.arange(0, MLA_ROPE_DIM, 2, dtype=jnp.float32) / MLA_ROPE_DIM))
    invf = jnp.zeros((1, HEAD_SLAB), jnp.float32)
    invf = invf.at[0, MLA_NOPE_DIM:MLA_NOPE_DIM + MLA_ROPE_DIM].set(jnp.concatenate([inv_freq, inv_freq]))

    x2 = x.reshape(b * s, d)
    q, k, v, dq, dk, dv = _proj(x2, positions.reshape(b * s, 1), invf, wcq, wckv, wkpe, wd, wuq, wukv,
                                mla_q_norm[layer][None, :], mla_kv_norm[layer][None, :])
    r3 = lambda a: a.reshape(b, s, a.shape[-1])
    o_mla = _mla_attention(r3(q), r3(k), r3(v))

    slopes = jnp.asarray(_alibi_slopes(DIFF_HEADS))
    o_diff = _diff_attention(slopes, r3(dq), r3(dk), r3(dv), positions.reshape(b, s, 1),
                             positions.reshape(b, s // ATTN_K, 1, ATTN_K),
                             diff_lambda_q1[layer][None, :], diff_lambda_k1[layer][None, :],
                             diff_lambda_q2[layer][None, :], diff_lambda_k2[layer][None, :],
                             diff_subln[layer][None, :])

    row = lambda a: a[layer][None, :]
    out = _post(x2, o_mla.reshape(b * s, -1), o_diff.reshape(b * s, -1), wg, row(b_gate),
                w_o_mla[layer].astype(bf), w_o_diff[layer].astype(bf), w_o[layer].astype(bf),
                row(ln1_g), row(ln1_b), w_up[layer].astype(bf), w_down[layer].astype(bf), row(ln2_g), row(ln2_b))
    return out.reshape(b, s, d)
```

```python
import functools
import math

import jax
import jax.numpy as jnp
import numpy as np
from jax import lax
from jax.experimental import pallas as pl
from jax.experimental.pallas import tpu as pltpu

D_MODEL = 1024
MLA_HEADS = 8
MLA_NOPE_DIM = 64
MLA_ROPE_DIM = 32
MLA_V_DIM = 64
MLA_Q_RANK = 384
MLA_KV_RANK = 256
ROPE_THETA = 10000.0
DIFF_HEADS = 8
DIFF_HEAD_DIM = 64
DIFF_V_DIM = 128
D_FF = 4 * D_MODEL
DEPTH = 1
LN_EPS = 1e-5
RMS_EPS = 1e-6
NEG_INF = -1e30
DEEPNORM_ALPHA = (2.0 * DEPTH) ** 0.25
LAMBDA_INIT = 0.8 - 0.6 * math.exp(-0.3 * 0)

LANES = 128
HEAD_SLAB = LANES
VMEM_LIMIT = 56 * 1024 * 1024

PROJ_ROWS = 512
LOG2E = math.log2(math.e)
ATTN_Q = 512
POST_ROWS = 256
FF_CHUNK = 1024

_O1 = MLA_Q_RANK
_O2 = _O1 + MLA_KV_RANK
_O3 = _O2 + MLA_ROPE_DIM
_O4 = _O3 + DIFF_HEADS * 2 * DIFF_HEAD_DIM
_O5 = _O4 + DIFF_HEADS * 2 * DIFF_HEAD_DIM
_O6 = _O5 + DIFF_HEADS * DIFF_V_DIM


def _alibi_slopes(n):
    def pow2_slopes(k):
        start = 2.0 ** (-8.0 / k)
        return [start ** (i + 1) for i in range(k)]
    if math.log2(n).is_integer():
        s = pow2_slopes(n)
    else:
        c = 2 ** int(math.floor(math.log2(n)))
        s = pow2_slopes(c) + pow2_slopes(2 * c)[0::2][: n - c]
    return np.asarray(s, dtype=np.float32)


def _resident(shape):
    return pl.BlockSpec(shape, lambda *_: (0,) * len(shape), pipeline_mode=pl.Buffered(1))


def _rms(x, g):
    ms = jnp.mean(x * x, axis=-1, keepdims=True)
    return x * lax.rsqrt(ms + RMS_EPS) * g


def _layer_norm(x, g, b):
    mu = jnp.mean(x, axis=-1, keepdims=True)
    xc = x - mu
    var = jnp.mean(xc * xc, axis=-1, keepdims=True)
    return xc * lax.rsqrt(var + LN_EPS) * g + b


def _dot(a, b):
    return jnp.dot(a, b, preferred_element_type=jnp.float32)


def _dot_nt(a, b):
    return lax.dot_general(a, b, (((1,), (1,)), ((), ())), preferred_element_type=jnp.float32)


def _proj_kernel(x_ref, pos_ref, invf_ref, wcq_ref, wckv_ref, wkpe_ref, wd_ref, wuq_ref, wukv_ref,
                 gq_ref, gkv_ref, q_ref, k_ref, v_ref, dq_ref, dk_ref, dv_ref):
    xb = x_ref[...].astype(jnp.bfloat16)
    ang = pos_ref[...].astype(jnp.float32) * invf_ref[...]
    cos = jnp.cos(ang)
    sin = jnp.sin(ang)

    cq = _rms(_dot(xb, wcq_ref[...]), gq_ref[...]).astype(jnp.bfloat16)
    qq = _dot(cq, wuq_ref[...])
    scale = (MLA_NOPE_DIM + MLA_ROPE_DIM) ** -0.5 * LOG2E
    cos_s = cos * scale
    sin_s = sin * scale
    half = MLA_HEADS * HEAD_SLAB
    for h in range(MLA_HEADS):
        lo = h * HEAD_SLAB
        q_ref[:, lo:lo + HEAD_SLAB] = (qq[:, lo:lo + HEAD_SLAB] * cos_s
                                       + qq[:, half + lo:half + lo + HEAD_SLAB] * sin_s).astype(q_ref.dtype)

    kp = _dot(xb, wkpe_ref[...])
    kpe = kp[:, :HEAD_SLAB] * cos + kp[:, HEAD_SLAB:] * sin

    ckv = _rms(_dot(xb, wckv_ref[...]), gkv_ref[...]).astype(jnp.bfloat16)
    kv = _dot(ckv, wukv_ref[...])
    for h in range(MLA_HEADS):
        lo = h * HEAD_SLAB
        k_ref[:, lo:lo + HEAD_SLAB] = (kv[:, lo:lo + HEAD_SLAB] + kpe).astype(k_ref.dtype)
    v_ref[...] = kv[:, half:].astype(v_ref.dtype)

    d = _dot(xb, wd_ref[...])
    n = DIFF_HEADS * 2 * DIFF_HEAD_DIM
    dq_ref[...] = (d[:, :n] * (DIFF_HEAD_DIM ** -0.5 * LOG2E)).astype(dq_ref.dtype)
    dk_ref[...] = d[:, n:2 * n].astype(dk_ref.dtype)
    dv_ref[...] = d[:, 2 * n:].astype(dv_ref.dtype)


def _proj(x2, pos_col, invf, wcq, wckv, wkpe, wd, wuq, wukv, gq, gkv):
    t = x2.shape[0]
    rows = PROJ_ROWS
    bf = jnp.bfloat16
    tok = lambda w: pl.BlockSpec((rows, w), lambda i: (i, 0))
    out_shape = [jax.ShapeDtypeStruct((t, MLA_HEADS * HEAD_SLAB), bf),
                 jax.ShapeDtypeStruct((t, MLA_HEADS * HEAD_SLAB), bf),
                 jax.ShapeDtypeStruct((t, MLA_HEADS * MLA_V_DIM), bf),
                 jax.ShapeDtypeStruct((t, D_MODEL), bf),
                 jax.ShapeDtypeStruct((t, D_MODEL), bf),
                 jax.ShapeDtypeStruct((t, D_MODEL), bf)]
    return pl.pallas_call(
        _proj_kernel,
        grid=(t // rows,),
        in_specs=[tok(D_MODEL), tok(1), _resident(invf.shape), _resident(wcq.shape), _resident(wckv.shape),
                  _resident(wkpe.shape), _resident(wd.shape), _resident(wuq.shape), _resident(wukv.shape),
                  _resident(gq.shape), _resident(gkv.shape)],
        out_specs=[tok(s.shape[1]) for s in out_shape],
        out_shape=out_shape,
        compiler_params=pltpu.CompilerParams(dimension_semantics=("parallel",), vmem_limit_bytes=VMEM_LIMIT),
        name="proj",
    )(x2, pos_col, invf, wcq, wckv, wkpe, wd, wuq, wukv, gq, gkv)


def _causal_mask(tq, tk):
    row = lax.broadcasted_iota(jnp.int32, (tq, tk), 0)
    col = lax.broadcasted_iota(jnp.int32, (tq, tk), 1)
    return col <= row


def _softmax_pv(s_diag, s_past, v_diag, v_past):
    m = jnp.max(s_diag, axis=-1, keepdims=True)
    if s_past is not None:
        m = jnp.maximum(m, jnp.max(s_past, axis=-1, keepdims=True))
    p = jnp.exp2(s_diag - m)
    l = jnp.sum(p, axis=-1, keepdims=True)
    acc = _dot(p.astype(jnp.bfloat16), v_diag)
    if s_past is not None:
        p = jnp.exp2(s_past - m)
        l = l + jnp.sum(p, axis=-1, keepdims=True)
        acc = acc + _dot(p.astype(jnp.bfloat16), v_past)
    return acc / l


def _mla_kernel(q_ref, k_ref, v_ref, o_ref):
    tq = ATTN_Q
    seq = q_ref.shape[1]
    mask = _causal_mask(tq, tq)
    lane = lax.broadcasted_iota(jnp.int32, (tq, 2 * MLA_V_DIM), 1)
    for i in range(seq // tq):
        r0 = i * tq
        outs = []
        for hh in range(2):
            lo = hh * HEAD_SLAB
            q = q_ref[0, r0:r0 + tq, lo:lo + HEAD_SLAB]
            s_d = jnp.where(mask, _dot_nt(q, k_ref[0, r0:r0 + tq, lo:lo + HEAD_SLAB]), NEG_INF)
            s_p = _dot_nt(q, k_ref[0, :r0, lo:lo + HEAD_SLAB]) if i > 0 else None
            v_p = v_ref[0, :r0, :] if i > 0 else None
            outs.append(_softmax_pv(s_d, s_p, v_ref[0, r0:r0 + tq, :], v_p))
        o_ref[0, r0:r0 + tq, :] = jnp.where(lane < MLA_V_DIM, outs[0], outs[1]).astype(o_ref.dtype)


def _mla_attention(q, k, v):
    b, s, _ = q.shape
    pairs = MLA_HEADS // 2
    blk = lambda w: pl.BlockSpec((1, s, w), lambda bi, p: (bi, 0, p))
    return pl.pallas_call(
        _mla_kernel,
        grid=(b, pairs),
        in_specs=[blk(2 * HEAD_SLAB), blk(2 * HEAD_SLAB), blk(2 * MLA_V_DIM)],
        out_specs=blk(2 * MLA_V_DIM),
        out_shape=jax.ShapeDtypeStruct((b, s, MLA_HEADS * MLA_V_DIM), jnp.bfloat16),
        compiler_params=pltpu.CompilerParams(dimension_semantics=("parallel", "parallel"),
                                             vmem_limit_bytes=VMEM_LIMIT),
        name="mla_attn",
    )(q, k, v)


def _diff_kernel(slopes_ref, q_ref, k_ref, v_ref, pq_ref, pk_ref, lq1_ref, lk1_ref, lq2_ref, lk2_ref, g_ref,
                 o_ref):
    tq = ATTN_Q
    seq = q_ref.shape[1]
    mask = _causal_mask(tq, tq)
    slope = slopes_ref[pl.program_id(1)] * LOG2E
    lane = lax.broadcasted_iota(jnp.int32, (tq, LANES), 1)
    lam = (jnp.exp(jnp.sum(lq1_ref[...] * lk1_ref[...], axis=-1, keepdims=True))
           - jnp.exp(jnp.sum(lq2_ref[...] * lk2_ref[...], axis=-1, keepdims=True)) + LAMBDA_INIT)
    for i in range(seq // tq):
        r0 = i * tq
        q = q_ref[0, r0:r0 + tq, :]
        zero = jnp.zeros_like(q)
        qs = (jnp.where(lane < DIFF_HEAD_DIM, q, zero), jnp.where(lane < DIFF_HEAD_DIM, zero, q))
        pq = pq_ref[0, r0:r0 + tq, :].astype(jnp.float32) * slope
        bias_d = jnp.abs(pq - pk_ref[0, :, r0:r0 + tq].astype(jnp.float32) * slope)
        k_d = k_ref[0, r0:r0 + tq, :]
        v_d = v_ref[0, r0:r0 + tq, :]
        if i > 0:
            bias_p = jnp.abs(pq - pk_ref[0, :, :r0].astype(jnp.float32) * slope)
            k_p = k_ref[0, :r0, :]
            v_p = v_ref[0, :r0, :]
        outs = []
        for qc in qs:
            s_d = jnp.where(mask, _dot_nt(qc, k_d) - bias_d, NEG_INF)
            s_p = _dot_nt(qc, k_p) - bias_p if i > 0 else None
            outs.append(_softmax_pv(s_d, s_p, v_d, v_p if i > 0 else None))
        o = outs[0] - lam * outs[1]
        o_ref[0, r0:r0 + tq, :] = (_rms(o, g_ref[...]) * (1.0 - LAMBDA_INIT)).astype(o_ref.dtype)


def _diff_attention(slopes, q, k, v, pos_col, pos_row, lq1, lk1, lq2, lk2, g):
    b, s, _ = q.shape
    vec = lambda a: pl.BlockSpec(a.shape, lambda bi, h: (0, 0))
    head = pl.BlockSpec((1, s, LANES), lambda bi, h: (bi, 0, h))
    return pl.pallas_call(
        _diff_kernel,
        grid=(b, DIFF_HEADS),
        in_specs=[pl.BlockSpec(memory_space=pltpu.SMEM), head, head, head,
                  pl.BlockSpec((1, s, 1), lambda bi, h: (bi, 0, 0)),
                  pl.BlockSpec((1, 1, s), lambda bi, h: (bi, 0, 0)),
                  vec(lq1), vec(lk1), vec(lq2), vec(lk2), vec(g)],
        out_specs=head,
        out_shape=jax.ShapeDtypeStruct((b, s, DIFF_HEADS * DIFF_V_DIM), jnp.bfloat16),
        compiler_params=pltpu.CompilerParams(dimension_semantics=("parallel", "parallel"),
                                             vmem_limit_bytes=VMEM_LIMIT),
        name="diff_attn",
    )(slopes, q, k, v, pos_col, pos_row, lq1, lk1, lq2, lk2, g)


def _post_kernel(x_ref, om_ref, od_ref, wg_ref, bg_ref, wom_ref, wod_ref, wo_ref, g1_ref, b1_ref,
                 wup_ref, wdn_ref, g2_ref, b2_ref, o_ref):
    x = x_ref[...]
    xb = x.astype(jnp.bfloat16)
    gate = jax.nn.sigmoid(_dot(xb, wg_ref[...]) + bg_ref[...])
    y_mla = _dot(om_ref[...], wom_ref[...])
    y_diff = _dot(od_ref[...], wod_ref[...])
    merged = gate[:, :D_MODEL] * y_mla + gate[:, D_MODEL:] * y_diff
    mixed = _dot(merged.astype(jnp.bfloat16), wo_ref[...])
    x1 = _layer_norm(DEEPNORM_ALPHA * x + mixed, g1_ref[...], b1_ref[...])
    x1b = x1.astype(jnp.bfloat16)
    acc = DEEPNORM_ALPHA * x1
    for c in range(D_FF // FF_CHUNK):
        lo = c * FF_CHUNK
        hid = jnp.maximum(_dot(x1b, wup_ref[:, lo:lo + FF_CHUNK]), 0.0)
        acc = acc + _dot((hid * hid).astype(jnp.bfloat16), wdn_ref[lo:lo + FF_CHUNK, :])
    o_ref[...] = _layer_norm(acc, g2_ref[...], b2_ref[...]).astype(o_ref.dtype)


def _post(x2, o_mla, o_diff, wg, bg, wom, wod, wo, g1, b1, wup, wdn, g2, b2):
    t = x2.shape[0]
    rows = POST_ROWS
    tok = lambda w: pl.BlockSpec((rows, w), lambda i: (i, 0))
    consts = (wg, bg, wom, wod, wo, g1, b1, wup, wdn, g2, b2)
    return pl.pallas_call(
        _post_kernel,
        grid=(t // rows,),
        in_specs=[tok(D_MODEL), tok(o_mla.shape[1]), tok(o_diff.shape[1])] + [_resident(a.shape) for a in consts],
        out_specs=tok(D_MODEL),
        out_shape=jax.ShapeDtypeStruct((t, D_MODEL), jnp.float32),
        compiler_params=pltpu.CompilerParams(dimension_semantics=("parallel",), vmem_limit_bytes=VMEM_LIMIT),
        name="post",
    )(x2, o_mla, o_diff, *consts)


def _rot_half_cols(w):
    half = w.shape[-1] // 2
    return jnp.concatenate([-w[..., half:], w[..., :half]], axis=-1)


def _prep_weights(w_in, w_uq, w_ukv):
    bf = jnp.bfloat16
    f32 = jnp.float32
    wcq = w_in[:, :_O1].astype(bf)
    wckv = w_in[:, _O1:_O2].astype(bf)
    wkpe_raw = w_in[:, _O2:_O3]
    wd = w_in[:, _O3:_O6].astype(bf)
    wg = w_in[:, _O6:].astype(bf)

    pad_r = HEAD_SLAB - MLA_NOPE_DIM - MLA_ROPE_DIM
    d = w_in.shape[0]
    slab = lambda w: jnp.concatenate([jnp.zeros((d, MLA_NOPE_DIM), f32), w, jnp.zeros((d, pad_r), f32)], axis=1)
    wkpe = jnp.concatenate([slab(wkpe_raw), slab(_rot_half_cols(wkpe_raw))], axis=1).astype(bf)

    r = w_uq.shape[0]
    uq = w_uq.reshape(r, MLA_HEADS, MLA_NOPE_DIM + MLA_ROPE_DIM)
    nope, pe = uq[..., :MLA_NOPE_DIM], uq[..., MLA_NOPE_DIM:]
    plain = jnp.concatenate([nope, pe, jnp.zeros((r, MLA_HEADS, pad_r), f32)], axis=-1)
    rot = jnp.concatenate([jnp.zeros_like(nope), _rot_half_cols(pe), jnp.zeros((r, MLA_HEADS, pad_r), f32)], axis=-1)
    wuq = jnp.concatenate([plain.reshape(r, -1), rot.reshape(r, -1)], axis=1).astype(bf)

    r = w_ukv.shape[0]
    ukv = w_ukv.reshape(r, MLA_HEADS, MLA_NOPE_DIM + MLA_V_DIM)
    kn, vv = ukv[..., :MLA_NOPE_DIM], ukv[..., MLA_NOPE_DIM:]
    kslab = jnp.concatenate([kn, jnp.zeros((r, MLA_HEADS, HEAD_SLAB - MLA_NOPE_DIM), f32)], axis=-1)
    wukv = jnp.concatenate([kslab.reshape(r, -1), vv.reshape(r, -1)], axis=1).astype(bf)
    return wcq, wckv, wkpe, wd, wg, wuq, wukv


def kernel(x, positions, w_in, b_gate, mla_q_norm, mla_kv_norm, w_uq, w_ukv, w_o_mla, diff_lambda_q1, diff_lambda_k1, diff_lambda_q2, diff_lambda_k2, diff_subln, w_o_diff, w_o, ln1_g, ln1_b, w_up, w_down, ln2_g, ln2_b):
    b, s, d = x.shape
    bf = jnp.bfloat16
    layer = 0
    wcq, wckv, wkpe, wd, wg, wuq, wukv = _prep_weights(w_in[layer], w_uq[layer], w_ukv[layer])

    inv_freq = 1.0 / (ROPE_THETA ** (jnp.arange(0, MLA_ROPE_DIM, 2, dtype=jnp.float32) / MLA_ROPE_DIM))
    invf = jnp.zeros((1, HEAD_SLAB), jnp.float32)
    invf = invf.at[0, MLA_NOPE_DIM:MLA_NOPE_DIM + MLA_ROPE_DIM].set(jnp.concatenate([inv_freq, inv_freq]))

    x2 = x.reshape(b * s, d)
    q, k, v, dq, dk, dv = _proj(x2, positions.reshape(b * s, 1), invf, wcq, wckv, wkpe, wd, wuq, wukv,
                                mla_q_norm[layer][None, :], mla_kv_norm[layer][None, :])
    r3 = lambda a: a.reshape(b, s, a.shape[-1])
    o_mla = _mla_attention(r3(q), r3(k), r3(v))

    slopes = jnp.asarray(_alibi_slopes(DIFF_HEADS))
    o_diff = _diff_attention(slopes, r3(dq), r3(dk), r3(dv), positions.reshape(b, s, 1),
                             positions.reshape(b, 1, s),
                             diff_lambda_q1[layer][None, :], diff_lambda_k1[layer][None, :],
                             diff_lambda_q2[layer][None, :], diff_lambda_k2[layer][None, :],
                             diff_subln[layer][None, :])

    row = lambda a: a[layer][None, :]
    out = _post(x2, o_mla.reshape(b * s, -1), o_diff.reshape(b * s, -1), wg, row(b_gate),
                w_o_mla[layer].astype(bf), w_o_diff[layer].astype(bf), w_o[layer].astype(bf),
                row(ln1_g), row(ln1_b), w_up[layer].astype(bf), w_down[layer].astype(bf), row(ln2_g), row(ln2_b))
    return out.reshape(b, s, d)
```

```python
import functools
import math

import jax
import jax.numpy as jnp
import numpy as np
from jax import lax
from jax.experimental import pallas as pl
from jax.experimental.pallas import tpu as pltpu

D_MODEL = 1024
MLA_HEADS = 8
MLA_NOPE_DIM = 64
MLA_ROPE_DIM = 32
MLA_V_DIM = 64
MLA_Q_RANK = 384
MLA_KV_RANK = 256
ROPE_THETA = 10000.0
DIFF_HEADS = 8
DIFF_HEAD_DIM = 64
DIFF_V_DIM = 128
D_FF = 4 * D_MODEL
DEPTH = 1
LN_EPS = 1e-5
RMS_EPS = 1e-6
NEG_INF = -1e30
DEEPNORM_ALPHA = (2.0 * DEPTH) ** 0.25
LAMBDA_INIT = 0.8 - 0.6 * math.exp(-0.3 * 0)

LANES = 128
HEAD_SLAB = LANES
VMEM_LIMIT = 56 * 1024 * 1024

PROJ_ROWS = 512
LOG2E = math.log2(math.e)
MLA_TILE = 512
DIFF_TILE = 256
POST_ROWS = 256
FF_CHUNK = 1024

_O1 = MLA_Q_RANK
_O2 = _O1 + MLA_KV_RANK
_O3 = _O2 + MLA_ROPE_DIM
_O4 = _O3 + DIFF_HEADS * 2 * DIFF_HEAD_DIM
_O5 = _O4 + DIFF_HEADS * 2 * DIFF_HEAD_DIM
_O6 = _O5 + DIFF_HEADS * DIFF_V_DIM


def _alibi_slopes(n):
    def pow2_slopes(k):
        start = 2.0 ** (-8.0 / k)
        return [start ** (i + 1) for i in range(k)]
    if math.log2(n).is_integer():
        s = pow2_slopes(n)
    else:
        c = 2 ** int(math.floor(math.log2(n)))
        s = pow2_slopes(c) + pow2_slopes(2 * c)[0::2][: n - c]
    return np.asarray(s, dtype=np.float32)


def _resident(shape):
    return pl.BlockSpec(shape, lambda *_: (0,) * len(shape), pipeline_mode=pl.Buffered(1))


def _rms(x, g):
    ms = jnp.mean(x * x, axis=-1, keepdims=True)
    return x * lax.rsqrt(ms + RMS_EPS) * g


def _layer_norm(x, g, b):
    mu = jnp.mean(x, axis=-1, keepdims=True)
    xc = x - mu
    var = jnp.mean(xc * xc, axis=-1, keepdims=True)
    return xc * lax.rsqrt(var + LN_EPS) * g + b


def _dot(a, b):
    return jnp.dot(a, b, preferred_element_type=jnp.float32)


def _dot_nt(a, b):
    return lax.dot_general(a, b, (((1,), (1,)), ((), ())), preferred_element_type=jnp.float32)


def _proj_kernel(x_ref, pos_ref, invf_ref, wcq_ref, wckv_ref, wkpe_ref, wd_ref, wuq_ref, wukv_ref,
                 gq_ref, gkv_ref, q_ref, k_ref, v_ref, dq_ref, dk_ref, dv_ref):
    xb = x_ref[...].astype(jnp.bfloat16)
    ang = pos_ref[...].astype(jnp.float32) * invf_ref[...]
    cos = jnp.cos(ang)
    sin = jnp.sin(ang)

    d = _dot(xb, wd_ref[...])
    n = DIFF_HEADS * 2 * DIFF_HEAD_DIM
    dq_ref[...] = (d[:, :n] * (DIFF_HEAD_DIM ** -0.5 * LOG2E)).astype(dq_ref.dtype)
    dk_ref[...] = d[:, n:2 * n].astype(dk_ref.dtype)
    dv_ref[...] = d[:, 2 * n:].astype(dv_ref.dtype)

    cq = _rms(_dot(xb, wcq_ref[...]), gq_ref[...]).astype(jnp.bfloat16)
    qq = _dot(cq, wuq_ref[...])
    scale = (MLA_NOPE_DIM + MLA_ROPE_DIM) ** -0.5 * LOG2E
    cos_s = cos * scale
    sin_s = sin * scale
    half = MLA_HEADS * HEAD_SLAB
    for h in range(MLA_HEADS):
        lo = h * HEAD_SLAB
        q_ref[:, lo:lo + HEAD_SLAB] = (qq[:, lo:lo + HEAD_SLAB] * cos_s
                                       + qq[:, half + lo:half + lo + HEAD_SLAB] * sin_s).astype(q_ref.dtype)

    kp = _dot(xb, wkpe_ref[...])
    kpe = kp[:, :HEAD_SLAB] * cos + kp[:, HEAD_SLAB:] * sin

    ckv = _rms(_dot(xb, wckv_ref[...]), gkv_ref[...]).astype(jnp.bfloat16)
    kv = _dot(ckv, wukv_ref[...])
    for h in range(MLA_HEADS):
        lo = h * HEAD_SLAB
        k_ref[:, lo:lo + HEAD_SLAB] = (kv[:, lo:lo + HEAD_SLAB] + kpe).astype(k_ref.dtype)
    v_ref[...] = kv[:, half:].astype(v_ref.dtype)


def _proj(x2, pos_col, invf, wcq, wckv, wkpe, wd, wuq, wukv, gq, gkv):
    t = x2.shape[0]
    rows = PROJ_ROWS
    bf = jnp.bfloat16
    tok = lambda w: pl.BlockSpec((rows, w), lambda i: (i, 0))
    out_shape = [jax.ShapeDtypeStruct((t, MLA_HEADS * HEAD_SLAB), bf),
                 jax.ShapeDtypeStruct((t, MLA_HEADS * HEAD_SLAB), bf),
                 jax.ShapeDtypeStruct((t, MLA_HEADS * MLA_V_DIM), bf),
                 jax.ShapeDtypeStruct((t, D_MODEL), bf),
                 jax.ShapeDtypeStruct((t, D_MODEL), bf),
                 jax.ShapeDtypeStruct((t, D_MODEL), bf)]
    return pl.pallas_call(
        _proj_kernel,
        grid=(t // rows,),
        in_specs=[tok(D_MODEL), tok(1), _resident(invf.shape), _resident(wcq.shape), _resident(wckv.shape),
                  _resident(wkpe.shape), _resident(wd.shape), _resident(wuq.shape), _resident(wukv.shape),
                  _resident(gq.shape), _resident(gkv.shape)],
        out_specs=[tok(s.shape[1]) for s in out_shape],
        out_shape=out_shape,
        compiler_params=pltpu.CompilerParams(dimension_semantics=("parallel",), vmem_limit_bytes=VMEM_LIMIT),
        name="proj",
    )(x2, pos_col, invf, wcq, wckv, wkpe, wd, wuq, wukv, gq, gkv)


def _causal_mask(tq, tk):
    row = lax.broadcasted_iota(jnp.int32, (tq, tk), 0)
    col = lax.broadcasted_iota(jnp.int32, (tq, tk), 1)
    return col <= row


def _softmax_pv(s_diag, s_past, v_diag, v_past):
    m = jnp.max(s_diag, axis=-1, keepdims=True)
    if s_past is not None:
        m = jnp.maximum(m, jnp.max(s_past, axis=-1, keepdims=True))
    acc = _dot(jnp.exp2(s_diag - m).astype(jnp.bfloat16), v_diag)
    if s_past is not None:
        acc = acc + _dot(jnp.exp2(s_past - m).astype(jnp.bfloat16), v_past)
    return acc[:, :LANES] / acc[:, LANES:]


def _fill_values(v_ref, vaug_ref):
    vaug_ref[:, :LANES] = v_ref[0]
    vaug_ref[:, LANES:] = jnp.ones((vaug_ref.shape[0], LANES), vaug_ref.dtype)


def _mla_kernel(q_ref, k_ref, v_ref, o_ref, vaug_ref):
    tq = MLA_TILE
    seq = q_ref.shape[1]
    _fill_values(v_ref, vaug_ref)
    mask = _causal_mask(tq, tq)
    lane = lax.broadcasted_iota(jnp.int32, (tq, 2 * MLA_V_DIM), 1)
    for i in reversed(range(seq // tq)):
        r0 = i * tq
        outs = []
        for hh in range(2):
            lo = hh * HEAD_SLAB
            q = q_ref[0, r0:r0 + tq, lo:lo + HEAD_SLAB]
            s_d = jnp.where(mask, _dot_nt(q, k_ref[0, r0:r0 + tq, lo:lo + HEAD_SLAB]), NEG_INF)
            s_p = _dot_nt(q, k_ref[0, :r0, lo:lo + HEAD_SLAB]) if i > 0 else None
            v_p = vaug_ref[:r0, :] if i > 0 else None
            outs.append(_softmax_pv(s_d, s_p, vaug_ref[r0:r0 + tq, :], v_p))
        o_ref[0, r0:r0 + tq, :] = jnp.where(lane < MLA_V_DIM, outs[0], outs[1]).astype(o_ref.dtype)


def _mla_attention(q, k, v):
    b, s, _ = q.shape
    pairs = MLA_HEADS // 2
    blk = lambda w: pl.BlockSpec((1, s, w), lambda bi, p: (bi, 0, p))
    return pl.pallas_call(
        _mla_kernel,
        grid=(b, pairs),
        in_specs=[blk(2 * HEAD_SLAB), blk(2 * HEAD_SLAB), blk(2 * MLA_V_DIM)],
        out_specs=blk(2 * MLA_V_DIM),
        out_shape=jax.ShapeDtypeStruct((b, s, MLA_HEADS * MLA_V_DIM), jnp.bfloat16),
        scratch_shapes=[pltpu.VMEM((s, 2 * LANES), jnp.bfloat16)],
        compiler_params=pltpu.CompilerParams(dimension_semantics=("parallel", "parallel"),
                                             vmem_limit_bytes=VMEM_LIMIT),
        name="mla_attn",
    )(q, k, v)


def _diff_kernel(slopes_ref, q_ref, k_ref, v_ref, pq_ref, pk_ref, lq1_ref, lk1_ref, lq2_ref, lk2_ref, g_ref,
                 o_ref, vaug_ref):
    tq = DIFF_TILE
    seq = q_ref.shape[1]
    _fill_values(v_ref, vaug_ref)
    mask = _causal_mask(tq, tq)
    slope = slopes_ref[pl.program_id(1)] * LOG2E
    lane = lax.broadcasted_iota(jnp.int32, (tq, LANES), 1)
    lam = (jnp.exp(jnp.sum(lq1_ref[...] * lk1_ref[...], axis=-1, keepdims=True))
           - jnp.exp(jnp.sum(lq2_ref[...] * lk2_ref[...], axis=-1, keepdims=True)) + LAMBDA_INIT)
    for i in reversed(range(seq // tq)):
        r0 = i * tq
        q = q_ref[0, r0:r0 + tq, :]
        zero = jnp.zeros_like(q)
        qq = jnp.concatenate([jnp.where(lane < DIFF_HEAD_DIM, q, zero), jnp.where(lane < DIFF_HEAD_DIM, zero, q)],
                             axis=0)
        both = lambda a: jnp.concatenate([a, a], axis=0)
        pq = pq_ref[0, r0:r0 + tq, :].astype(jnp.float32) * slope
        bias_d = jnp.abs(pq - pk_ref[0, :, r0:r0 + tq].astype(jnp.float32) * slope)
        s_d = _dot_nt(qq, k_ref[0, r0:r0 + tq, :]) - both(bias_d)
        s_d = jnp.where(both(mask), s_d, NEG_INF)
        s_p = v_p = None
        if i > 0:
            bias_p = jnp.abs(pq - pk_ref[0, :, :r0].astype(jnp.float32) * slope)
            s_p = _dot_nt(qq, k_ref[0, :r0, :]) - both(bias_p)
            v_p = vaug_ref[:r0, :]
        out = _softmax_pv(s_d, s_p, vaug_ref[r0:r0 + tq, :], v_p)
        o = out[:tq] - lam * out[tq:]
        o_ref[0, r0:r0 + tq, :] = (_rms(o, g_ref[...]) * (1.0 - LAMBDA_INIT)).astype(o_ref.dtype)


def _diff_attention(slopes, q, k, v, pos_col, pos_row, lq1, lk1, lq2, lk2, g):
    b, s, _ = q.shape
    vec = lambda a: pl.BlockSpec(a.shape, lambda bi, h: (0, 0))
    head = pl.BlockSpec((1, s, LANES), lambda bi, h: (bi, 0, h))
    return pl.pallas_call(
        _diff_kernel,
        grid=(b, DIFF_HEADS),
        in_specs=[pl.BlockSpec(memory_space=pltpu.SMEM), head, head, head,
                  pl.BlockSpec((1, s, 1), lambda bi, h: (bi, 0, 0)),
                  pl.BlockSpec((1, 1, s), lambda bi, h: (bi, 0, 0)),
                  vec(lq1), vec(lk1), vec(lq2), vec(lk2), vec(g)],
        out_specs=head,
        out_shape=jax.ShapeDtypeStruct((b, s, DIFF_HEADS * DIFF_V_DIM), jnp.bfloat16),
        scratch_shapes=[pltpu.VMEM((s, 2 * LANES), jnp.bfloat16)],
        compiler_params=pltpu.CompilerParams(dimension_semantics=("parallel", "parallel"),
                                             vmem_limit_bytes=VMEM_LIMIT),
        name="diff_attn",
    )(slopes, q, k, v, pos_col, pos_row, lq1, lk1, lq2, lk2, g)


def _post_kernel(x_ref, om_ref, od_ref, wg_ref, bg_ref, wom_ref, wod_ref, wo_ref, g1_ref, b1_ref,
                 wup_ref, wdn_ref, g2_ref, b2_ref, o_ref):
    x = x_ref[...]
    xb = x.astype(jnp.bfloat16)
    gate = jax.nn.sigmoid(_dot(xb, wg_ref[...]) + bg_ref[...])
    y_mla = _dot(om_ref[...], wom_ref[...])
    y_diff = _dot(od_ref[...], wod_ref[...])
    merged = gate[:, :D_MODEL] * y_mla + gate[:, D_MODEL:] * y_diff
    mixed = _dot(merged.astype(jnp.bfloat16), wo_ref[...])
    x1 = _layer_norm(DEEPNORM_ALPHA * x + mixed, g1_ref[...], b1_ref[...])
    x1b = x1.astype(jnp.bfloat16)
    acc = DEEPNORM_ALPHA * x1
    for c in range(D_FF // FF_CHUNK):
        lo = c * FF_CHUNK
        hid = jnp.maximum(_dot(x1b, wup_ref[:, lo:lo + FF_CHUNK]), 0.0)
        acc = acc + _dot((hid * hid).astype(jnp.bfloat16), wdn_ref[lo:lo + FF_CHUNK, :])
    o_ref[...] = _layer_norm(acc, g2_ref[...], b2_ref[...]).astype(o_ref.dtype)


def _post(x2, o_mla, o_diff, wg, bg, wom, wod, wo, g1, b1, wup, wdn, g2, b2):
    t = x2.shape[0]
    rows = POST_ROWS
    tok = lambda w: pl.BlockSpec((rows, w), lambda i: (i, 0))
    consts = (wg, bg, wom, wod, wo, g1, b1, wup, wdn, g2, b2)
    return pl.pallas_call(
        _post_kernel,
        grid=(t // rows,),
        in_specs=[tok(D_MODEL), tok(o_mla.shape[1]), tok(o_diff.shape[1])] + [_resident(a.shape) for a in consts],
        out_specs=tok(D_MODEL),
        out_shape=jax.ShapeDtypeStruct((t, D_MODEL), jnp.float32),
        compiler_params=pltpu.CompilerParams(dimension_semantics=("parallel",), vmem_limit_bytes=VMEM_LIMIT),
        name="post",
    )(x2, o_mla, o_diff, *consts)


def _rot_half_cols(w):
    half = w.shape[-1] // 2
    return jnp.concatenate([-w[..., half:], w[..., :half]], axis=-1)


def _prep_weights(w_in, w_uq, w_ukv):
    bf = jnp.bfloat16
    f32 = jnp.float32
    wcq = w_in[:, :_O1].astype(bf)
    wckv = w_in[:, _O1:_O2].astype(bf)
    wkpe_raw = w_in[:, _O2:_O3]
    wd = w_in[:, _O3:_O6].astype(bf)
    wg = w_in[:, _O6:].astype(bf)

    pad_r = HEAD_SLAB - MLA_NOPE_DIM - MLA_ROPE_DIM
    d = w_in.shape[0]
    slab = lambda w: jnp.concatenate([jnp.zeros((d, MLA_NOPE_DIM), f32), w, jnp.zeros((d, pad_r), f32)], axis=1)
    wkpe = jnp.concatenate([slab(wkpe_raw), slab(_rot_half_cols(wkpe_raw))], axis=1).astype(bf)

    r = w_uq.shape[0]
    uq = w_uq.reshape(r, MLA_HEADS, MLA_NOPE_DIM + MLA_ROPE_DIM)
    nope, pe = uq[..., :MLA_NOPE_DIM], uq[..., MLA_NOPE_DIM:]
    plain = jnp.concatenate([nope, pe, jnp.zeros((r, MLA_HEADS, pad_r), f32)], axis=-1)
    rot = jnp.concatenate([jnp.zeros_like(nope), _rot_half_cols(pe), jnp.zeros((r, MLA_HEADS, pad_r), f32)], axis=-1)
    wuq = jnp.concatenate([plain.reshape(r, -1), rot.reshape(r, -1)], axis=1).astype(bf)

    r = w_ukv.shape[0]
    ukv = w_ukv.reshape(r, MLA_HEADS, MLA_NOPE_DIM + MLA_V_DIM)
    kn, vv = ukv[..., :MLA_NOPE_DIM], ukv[..., MLA_NOPE_DIM:]
    kslab = jnp.concatenate([kn, jnp.zeros((r, MLA_HEADS, HEAD_SLAB - MLA_NOPE_DIM), f32)], axis=-1)
    wukv = jnp.concatenate([kslab.reshape(r, -1), vv.reshape(r, -1)], axis=1).astype(bf)
    return wcq, wckv, wkpe, wd, wg, wuq, wukv


def kernel(x, positions, w_in, b_gate, mla_q_norm, mla_kv_norm, w_uq, w_ukv, w_o_mla, diff_lambda_q1, diff_lambda_k1, diff_lambda_q2, diff_lambda_k2, diff_subln, w_o_diff, w_o, ln1_g, ln1_b, w_up, w_down, ln2_g, ln2_b):
    b, s, d = x.shape
    bf = jnp.bfloat16
    layer = 0
    wcq, wckv, wkpe, wd, wg, wuq, wukv = _prep_weights(w_in[layer], w_uq[layer], w_ukv[layer])

    inv_freq = 1.0 / (ROPE_THETA ** (jnp.arange(0, MLA_ROPE_DIM, 2, dtype=jnp.float32) / MLA_ROPE_DIM))
    invf = jnp.zeros((1, HEAD_SLAB), jnp.float32)
    invf = invf.at[0, MLA_NOPE_DIM:MLA_NOPE_DIM + MLA_ROPE_DIM].set(jnp.concatenate([inv_freq, inv_freq]))

    x2 = x.reshape(b * s, d)
    q, k, v, dq, dk, dv = _proj(x2, positions.reshape(b * s, 1), invf, wcq, wckv, wkpe, wd, wuq, wukv,
                                mla_q_norm[layer][None, :], mla_kv_norm[layer][None, :])
    r3 = lambda a: a.reshape(b, s, a.shape[-1])
    o_mla = _mla_attention(r3(q), r3(k), r3(v))

    slopes = jnp.asarray(_alibi_slopes(DIFF_HEADS))
    o_diff = _diff_attention(slopes, r3(dq), r3(dk), r3(dv), positions.reshape(b, s, 1),
                             positions.reshape(b, 1, s),
                             diff_lambda_q1[layer][None, :], diff_lambda_k1[layer][None, :],
                             diff_lambda_q2[layer][None, :], diff_lambda_k2[layer][None, :],
                             diff_subln[layer][None, :])

    row = lambda a: a[layer][None, :]
    out = _post(x2, o_mla.reshape(b * s, -1), o_diff.reshape(b * s, -1), wg, row(b_gate),
                w_o_mla[layer].astype(bf), w_o_diff[layer].astype(bf), w_o[layer].astype(bf),
                row(ln1_g), row(ln1_b), w_up[layer].astype(bf), w_down[layer].astype(bf), row(ln2_g), row(ln2_b))
    return out.reshape(b, s, d)
```

```python
import functools
import math

import jax
import jax.numpy as jnp
import numpy as np
from jax import lax
from jax.experimental import pallas as pl
from jax.experimental.pallas import tpu as pltpu

D_MODEL = 1024
MLA_HEADS = 8
MLA_NOPE_DIM = 64
MLA_ROPE_DIM = 32
MLA_V_DIM = 64
MLA_Q_RANK = 384
MLA_KV_RANK = 256
ROPE_THETA = 10000.0
DIFF_HEADS = 8
DIFF_HEAD_DIM = 64
DIFF_V_DIM = 128
D_FF = 4 * D_MODEL
DEPTH = 1
LN_EPS = 1e-5
RMS_EPS = 1e-6
NEG_INF = -1e30
DEEPNORM_ALPHA = (2.0 * DEPTH) ** 0.25
LAMBDA_INIT = 0.8 - 0.6 * math.exp(-0.3 * 0)

LANES = 128
HEAD_SLAB = LANES
VMEM_LIMIT = 56 * 1024 * 1024

PROJ_ROWS = 512
LOG2E = math.log2(math.e)
PIECES = 3
F32_EXACT_INT = 1 << 24
MLA_TILE = 512
DIFF_TILE = 256
POST_ROWS = 256
FF_CHUNK = 1024

_O1 = MLA_Q_RANK
_O2 = _O1 + MLA_KV_RANK
_O3 = _O2 + MLA_ROPE_DIM
_O4 = _O3 + DIFF_HEADS * 2 * DIFF_HEAD_DIM
_O5 = _O4 + DIFF_HEADS * 2 * DIFF_HEAD_DIM
_O6 = _O5 + DIFF_HEADS * DIFF_V_DIM


def _alibi_slopes(n):
    def pow2_slopes(k):
        start = 2.0 ** (-8.0 / k)
        return [start ** (i + 1) for i in range(k)]
    if math.log2(n).is_integer():
        s = pow2_slopes(n)
    else:
        c = 2 ** int(math.floor(math.log2(n)))
        s = pow2_slopes(c) + pow2_slopes(2 * c)[0::2][: n - c]
    return np.asarray(s, dtype=np.float32)


def _resident(shape):
    return pl.BlockSpec(shape, lambda *_: (0,) * len(shape), pipeline_mode=pl.Buffered(1))


def _rms(x, g):
    ms = jnp.mean(x * x, axis=-1, keepdims=True)
    return x * lax.rsqrt(ms + RMS_EPS) * g


def _layer_norm(x, g, b):
    mu = jnp.mean(x, axis=-1, keepdims=True)
    xc = x - mu
    var = jnp.mean(xc * xc, axis=-1, keepdims=True)
    return xc * lax.rsqrt(var + LN_EPS) * g + b


def _dot(a, b):
    return jnp.dot(a, b, preferred_element_type=jnp.float32)


def _dot_nt(a, b):
    return lax.dot_general(a, b, (((1,), (1,)), ((), ())), preferred_element_type=jnp.float32)


def _bf16_pieces(x):
    out = []
    for _ in range(PIECES):
        p = x.astype(jnp.bfloat16).astype(jnp.float32)
        out.append(p)
        x = x - p
    return out


def _piece_lanes(pieces, by_group):
    lane = lax.broadcasted_iota(jnp.int32, (1, LANES), 1)
    out = 0.0
    for a in range(PIECES):
        for b in range(PIECES):
            out = out + jnp.where(lane == PIECES * a + b, 1.0, 0.0) * pieces[a if by_group else b]
    return out


def _proj_kernel(x_ref, pos_ref, invf_ref, wcq_ref, wckv_ref, wkpe_ref, wd_ref, wuq_ref, wukv_ref,
                 gq_ref, gkv_ref, q_ref, k_ref, v_ref, dq_ref, dk_ref, dv_ref, kpos_ref):
    xb = x_ref[...].astype(jnp.bfloat16)
    pos = pos_ref[...].astype(jnp.float32)
    ang = pos * invf_ref[...]
    cos = jnp.cos(ang)
    sin = jnp.sin(ang)
    kpos_ref[...] = _piece_lanes(_bf16_pieces(pos), by_group=False).astype(kpos_ref.dtype)

    d = _dot(xb, wd_ref[...])
    n = DIFF_HEADS * 2 * DIFF_HEAD_DIM
    dq_ref[...] = (d[:, :n] * (DIFF_HEAD_DIM ** -0.5 * LOG2E)).astype(dq_ref.dtype)
    dk_ref[...] = d[:, n:2 * n].astype(dk_ref.dtype)
    dv_ref[...] = d[:, 2 * n:].astype(dv_ref.dtype)

    cq = _rms(_dot(xb, wcq_ref[...]), gq_ref[...]).astype(jnp.bfloat16)
    qq = _dot(cq, wuq_ref[...])
    scale = (MLA_NOPE_DIM + MLA_ROPE_DIM) ** -0.5 * LOG2E
    cos_s = cos * scale
    sin_s = sin * scale
    half = MLA_HEADS * HEAD_SLAB
    for h in range(MLA_HEADS):
        lo = h * HEAD_SLAB
        q_ref[:, lo:lo + HEAD_SLAB] = (qq[:, lo:lo + HEAD_SLAB] * cos_s
                                       + qq[:, half + lo:half + lo + HEAD_SLAB] * sin_s).astype(q_ref.dtype)

    kp = _dot(xb, wkpe_ref[...])
    kpe = kp[:, :HEAD_SLAB] * cos + kp[:, HEAD_SLAB:] * sin

    ckv = _rms(_dot(xb, wckv_ref[...]), gkv_ref[...]).astype(jnp.bfloat16)
    kv = _dot(ckv, wukv_ref[...])
    for h in range(MLA_HEADS):
        lo = h * HEAD_SLAB
        k_ref[:, lo:lo + HEAD_SLAB] = (kv[:, lo:lo + HEAD_SLAB] + kpe).astype(k_ref.dtype)
    v_ref[...] = kv[:, half:].astype(v_ref.dtype)


def _proj(x2, pos_col, invf, wcq, wckv, wkpe, wd, wuq, wukv, gq, gkv):
    t = x2.shape[0]
    rows = PROJ_ROWS
    bf = jnp.bfloat16
    tok = lambda w: pl.BlockSpec((rows, w), lambda i: (i, 0))
    out_shape = [jax.ShapeDtypeStruct((t, MLA_HEADS * HEAD_SLAB), bf),
                 jax.ShapeDtypeStruct((t, MLA_HEADS * HEAD_SLAB), bf),
                 jax.ShapeDtypeStruct((t, MLA_HEADS * MLA_V_DIM), bf),
                 jax.ShapeDtypeStruct((t, D_MODEL), bf),
                 jax.ShapeDtypeStruct((t, D_MODEL), bf),
                 jax.ShapeDtypeStruct((t, D_MODEL), bf),
                 jax.ShapeDtypeStruct((t, LANES), bf)]
    return pl.pallas_call(
        _proj_kernel,
        grid=(t // rows,),
        in_specs=[tok(D_MODEL), tok(1), _resident(invf.shape), _resident(wcq.shape), _resident(wckv.shape),
                  _resident(wkpe.shape), _resident(wd.shape), _resident(wuq.shape), _resident(wukv.shape),
                  _resident(gq.shape), _resident(gkv.shape)],
        out_specs=[tok(s.shape[1]) for s in out_shape],
        out_shape=out_shape,
        compiler_params=pltpu.CompilerParams(dimension_semantics=("parallel",), vmem_limit_bytes=VMEM_LIMIT),
        name="proj",
    )(x2, pos_col, invf, wcq, wckv, wkpe, wd, wuq, wukv, gq, gkv)


def _causal_mask(tq, tk):
    row = lax.broadcasted_iota(jnp.int32, (tq, tk), 0)
    col = lax.broadcasted_iota(jnp.int32, (tq, tk), 1)
    return col <= row


def _softmax_pv(s_diag, s_past, v_diag, v_past):
    m = jnp.max(s_diag, axis=-1, keepdims=True)
    if s_past is not None:
        m = jnp.maximum(m, jnp.max(s_past, axis=-1, keepdims=True))
    acc = _dot(jnp.exp2(s_diag - m).astype(jnp.bfloat16), v_diag)
    if s_past is not None:
        acc = acc + _dot(jnp.exp2(s_past - m).astype(jnp.bfloat16), v_past)
    return acc[:, :LANES] / acc[:, LANES:]


def _fill_values(v_ref, vaug_ref):
    vaug_ref[:, :LANES] = v_ref[0]
    vaug_ref[:, LANES:] = jnp.ones((vaug_ref.shape[0], LANES), vaug_ref.dtype)


def _mla_kernel(q_ref, k_ref, v_ref, o_ref, vaug_ref):
    tq = MLA_TILE
    seq = q_ref.shape[1]
    _fill_values(v_ref, vaug_ref)
    mask = _causal_mask(tq, tq)
    lane = lax.broadcasted_iota(jnp.int32, (tq, 2 * MLA_V_DIM), 1)
    for i in reversed(range(seq // tq)):
        r0 = i * tq
        outs = []
        for hh in range(2):
            lo = hh * HEAD_SLAB
            q = q_ref[0, r0:r0 + tq, lo:lo + HEAD_SLAB]
            s_d = jnp.where(mask, _dot_nt(q, k_ref[0, r0:r0 + tq, lo:lo + HEAD_SLAB]), NEG_INF)
            s_p = _dot_nt(q, k_ref[0, :r0, lo:lo + HEAD_SLAB]) if i > 0 else None
            v_p = vaug_ref[:r0, :] if i > 0 else None
            outs.append(_softmax_pv(s_d, s_p, vaug_ref[r0:r0 + tq, :], v_p))
        o_ref[0, r0:r0 + tq, :] = jnp.where(lane < MLA_V_DIM, outs[0], outs[1]).astype(o_ref.dtype)


def _mla_attention(q, k, v):
    b, s, _ = q.shape
    pairs = MLA_HEADS // 2
    blk = lambda w: pl.BlockSpec((1, s, w), lambda bi, p: (bi, 0, p))
    return pl.pallas_call(
        _mla_kernel,
        grid=(b, pairs),
        in_specs=[blk(2 * HEAD_SLAB), blk(2 * HEAD_SLAB), blk(2 * MLA_V_DIM)],
        out_specs=blk(2 * MLA_V_DIM),
        out_shape=jax.ShapeDtypeStruct((b, s, MLA_HEADS * MLA_V_DIM), jnp.bfloat16),
        scratch_shapes=[pltpu.VMEM((s, 2 * LANES), jnp.bfloat16)],
        compiler_params=pltpu.CompilerParams(dimension_semantics=("parallel", "parallel"),
                                             vmem_limit_bytes=VMEM_LIMIT),
        name="mla_attn",
    )(q, k, v)


def _diff_kernel(slopes_ref, ok_ref, q_ref, k_ref, v_ref, kpos_ref, pq_ref, pk_ref, lq1_ref, lk1_ref, lq2_ref,
                 lk2_ref, g_ref, o_ref, vaug_ref, kaug_ref):
    tq = DIFF_TILE
    seq = q_ref.shape[1]
    _fill_values(v_ref, vaug_ref)
    mask = _causal_mask(tq, tq)
    both = lambda a: jnp.concatenate([a, a], axis=0)
    mask2 = both(mask)
    slope = slopes_ref[pl.program_id(1)] * LOG2E
    lane = lax.broadcasted_iota(jnp.int32, (tq, LANES), 1)
    lam = (jnp.exp(jnp.sum(lq1_ref[...] * lk1_ref[...], axis=-1, keepdims=True))
           - jnp.exp(jnp.sum(lq2_ref[...] * lk2_ref[...], axis=-1, keepdims=True)) + LAMBDA_INIT)

    def tiles(fast):
        if fast:
            kaug_ref[:, :LANES] = k_ref[0]
            kaug_ref[:, LANES:] = kpos_ref[0]
            coef = _piece_lanes(_bf16_pieces(jnp.full((1, 1), slope, jnp.float32)), by_group=True)
            coef = jnp.broadcast_to(coef.astype(jnp.bfloat16), (2 * tq, LANES))
        for i in reversed(range(seq // tq)):
            r0 = i * tq
            q = q_ref[0, r0:r0 + tq, :]
            zero = jnp.zeros_like(q)
            qq = jnp.concatenate([jnp.where(lane < DIFF_HEAD_DIM, q, zero),
                                  jnp.where(lane < DIFF_HEAD_DIM, zero, q)], axis=0)
            if fast:
                qq = jnp.concatenate([qq, coef], axis=1)

                def scores(lo, hi):
                    return _dot_nt(qq, kaug_ref[lo:hi, :])
            else:
                pq = pq_ref[0, r0:r0 + tq, :]

                def scores(lo, hi):
                    dist = jnp.abs(pq - pk_ref[0, :, lo:hi]).astype(jnp.float32)
                    return _dot_nt(qq, k_ref[0, lo:hi, :]) - both(dist * slope)

            s_d = jnp.where(mask2, scores(r0, r0 + tq), NEG_INF)
            s_p = scores(0, r0) if i > 0 else None
            v_p = vaug_ref[:r0, :] if i > 0 else None
            out = _softmax_pv(s_d, s_p, vaug_ref[r0:r0 + tq, :], v_p)
            o = out[:tq] - lam * out[tq:]
            o_ref[0, r0:r0 + tq, :] = (_rms(o, g_ref[...]) * (1.0 - LAMBDA_INIT)).astype(o_ref.dtype)

    ok = ok_ref[pl.program_id(0)]
    pl.when(ok == 1)(functools.partial(tiles, True))
    pl.when(ok != 1)(functools.partial(tiles, False))


def _diff_attention(slopes, ok, q, k, v, kpos, pos_col, pos_row, lq1, lk1, lq2, lk2, g):
    b, s, _ = q.shape
    vec = lambda a: pl.BlockSpec(a.shape, lambda bi, h: (0, 0))
    head = pl.BlockSpec((1, s, LANES), lambda bi, h: (bi, 0, h))
    smem = pl.BlockSpec(memory_space=pltpu.SMEM)
    return pl.pallas_call(
        _diff_kernel,
        grid=(b, DIFF_HEADS),
        in_specs=[smem, smem, head, head, head,
                  pl.BlockSpec((1, s, LANES), lambda bi, h: (bi, 0, 0)),
                  pl.BlockSpec((1, s, 1), lambda bi, h: (bi, 0, 0)),
                  pl.BlockSpec((1, 1, s), lambda bi, h: (bi, 0, 0)),
                  vec(lq1), vec(lk1), vec(lq2), vec(lk2), vec(g)],
        out_specs=head,
        out_shape=jax.ShapeDtypeStruct((b, s, DIFF_HEADS * DIFF_V_DIM), jnp.bfloat16),
        scratch_shapes=[pltpu.VMEM((s, 2 * LANES), jnp.bfloat16), pltpu.VMEM((s, 2 * LANES), jnp.bfloat16)],
        compiler_params=pltpu.CompilerParams(dimension_semantics=("parallel", "parallel"),
                                             vmem_limit_bytes=VMEM_LIMIT),
        name="diff_attn",
    )(slopes, ok, q, k, v, kpos, pos_col, pos_row, lq1, lk1, lq2, lk2, g)


def _post_kernel(x_ref, om_ref, od_ref, wg_ref, bg_ref, wom_ref, wod_ref, wo_ref, g1_ref, b1_ref,
                 wup_ref, wdn_ref, g2_ref, b2_ref, o_ref):
    x = x_ref[...]
    xb = x.astype(jnp.bfloat16)
    gate = jax.nn.sigmoid(_dot(xb, wg_ref[...]) + bg_ref[...])
    y_mla = _dot(om_ref[...], wom_ref[...])
    y_diff = _dot(od_ref[...], wod_ref[...])
    merged = gate[:, :D_MODEL] * y_mla + gate[:, D_MODEL:] * y_diff
    mixed = _dot(merged.astype(jnp.bfloat16), wo_ref[...])
    x1 = _layer_norm(DEEPNORM_ALPHA * x + mixed, g1_ref[...], b1_ref[...])
    x1b = x1.astype(jnp.bfloat16)
    acc = DEEPNORM_ALPHA * x1
    for c in range(D_FF // FF_CHUNK):
        lo = c * FF_CHUNK
        hid = jnp.maximum(_dot(x1b, wup_ref[:, lo:lo + FF_CHUNK]), 0.0)
        acc = acc + _dot((hid * hid).astype(jnp.bfloat16), wdn_ref[lo:lo + FF_CHUNK, :])
    o_ref[...] = _layer_norm(acc, g2_ref[...], b2_ref[...]).astype(o_ref.dtype)


def _post(x2, o_mla, o_diff, wg, bg, wom, wod, wo, g1, b1, wup, wdn, g2, b2):
    t = x2.shape[0]
    rows = POST_ROWS
    tok = lambda w: pl.BlockSpec((rows, w), lambda i: (i, 0))
    consts = (wg, bg, wom, wod, wo, g1, b1, wup, wdn, g2, b2)
    return pl.pallas_call(
        _post_kernel,
        grid=(t // rows,),
        in_specs=[tok(D_MODEL), tok(o_mla.shape[1]), tok(o_diff.shape[1])] + [_resident(a.shape) for a in consts],
        out_specs=tok(D_MODEL),
        out_shape=jax.ShapeDtypeStruct((t, D_MODEL), jnp.float32),
        compiler_params=pltpu.CompilerParams(dimension_semantics=("parallel",), vmem_limit_bytes=VMEM_LIMIT),
        name="post",
    )(x2, o_mla, o_diff, *consts)


def _rot_half_cols(w):
    half = w.shape[-1] // 2
    return jnp.concatenate([-w[..., half:], w[..., :half]], axis=-1)


def _prep_weights(w_in, w_uq, w_ukv):
    bf = jnp.bfloat16
    f32 = jnp.float32
    wcq = w_in[:, :_O1].astype(bf)
    wckv = w_in[:, _O1:_O2].astype(bf)
    wkpe_raw = w_in[:, _O2:_O3]
    wd = w_in[:, _O3:_O6].astype(bf)
    wg = w_in[:, _O6:].astype(bf)

    pad_r = HEAD_SLAB - MLA_NOPE_DIM - MLA_ROPE_DIM
    d = w_in.shape[0]
    slab = lambda w: jnp.concatenate([jnp.zeros((d, MLA_NOPE_DIM), f32), w, jnp.zeros((d, pad_r), f32)], axis=1)
    wkpe = jnp.concatenate([slab(wkpe_raw), slab(_rot_half_cols(wkpe_raw))], axis=1).astype(bf)

    r = w_uq.shape[0]
    uq = w_uq.reshape(r, MLA_HEADS, MLA_NOPE_DIM + MLA_ROPE_DIM)
    nope, pe = uq[..., :MLA_NOPE_DIM], uq[..., MLA_NOPE_DIM:]
    plain = jnp.concatenate([nope, pe, jnp.zeros((r, MLA_HEADS, pad_r), f32)], axis=-1)
    rot = jnp.concatenate([jnp.zeros_like(nope), _rot_half_cols(pe), jnp.zeros((r, MLA_HEADS, pad_r), f32)], axis=-1)
    wuq = jnp.concatenate([plain.reshape(r, -1), rot.reshape(r, -1)], axis=1).astype(bf)

    r = w_ukv.shape[0]
    ukv = w_ukv.reshape(r, MLA_HEADS, MLA_NOPE_DIM + MLA_V_DIM)
    kn, vv = ukv[..., :MLA_NOPE_DIM], ukv[..., MLA_NOPE_DIM:]
    kslab = jnp.concatenate([kn, jnp.zeros((r, MLA_HEADS, HEAD_SLAB - MLA_NOPE_DIM), f32)], axis=-1)
    wukv = jnp.concatenate([kslab.reshape(r, -1), vv.reshape(r, -1)], axis=1).astype(bf)
    return wcq, wckv, wkpe, wd, wg, wuq, wukv


def kernel(x, positions, w_in, b_gate, mla_q_norm, mla_kv_norm, w_uq, w_ukv, w_o_mla, diff_lambda_q1, diff_lambda_k1, diff_lambda_q2, diff_lambda_k2, diff_subln, w_o_diff, w_o, ln1_g, ln1_b, w_up, w_down, ln2_g, ln2_b):
    b, s, d = x.shape
    bf = jnp.bfloat16
    layer = 0
    wcq, wckv, wkpe, wd, wg, wuq, wukv = _prep_weights(w_in[layer], w_uq[layer], w_ukv[layer])

    inv_freq = 1.0 / (ROPE_THETA ** (jnp.arange(0, MLA_ROPE_DIM, 2, dtype=jnp.float32) / MLA_ROPE_DIM))
    invf = jnp.zeros((1, HEAD_SLAB), jnp.float32)
    invf = invf.at[0, MLA_NOPE_DIM:MLA_NOPE_DIM + MLA_ROPE_DIM].set(jnp.concatenate([inv_freq, inv_freq]))

    x2 = x.reshape(b * s, d)
    q, k, v, dq, dk, dv, kpos = _proj(x2, positions.reshape(b * s, 1), invf, wcq, wckv, wkpe, wd, wuq, wukv,
                                      mla_q_norm[layer][None, :], mla_kv_norm[layer][None, :])
    r3 = lambda a: a.reshape(b, s, a.shape[-1])
    o_mla = _mla_attention(r3(q), r3(k), r3(v))

    slopes = jnp.asarray(_alibi_slopes(DIFF_HEADS))
    ordered = jnp.all(positions[:, 1:] >= positions[:, :-1], axis=1)
    exact_f32 = jnp.all((positions < F32_EXACT_INT) & (positions > -F32_EXACT_INT), axis=1)
    ok = (ordered & exact_f32).astype(jnp.int32)
    o_diff = _diff_attention(slopes, ok, r3(dq), r3(dk), r3(dv), r3(kpos), positions.reshape(b, s, 1),
                             positions.reshape(b, 1, s),
                             diff_lambda_q1[layer][None, :], diff_lambda_k1[layer][None, :],
                             diff_lambda_q2[layer][None, :], diff_lambda_k2[layer][None, :],
                             diff_subln[layer][None, :])

    row = lambda a: a[layer][None, :]
    out = _post(x2, o_mla.reshape(b * s, -1), o_diff.reshape(b * s, -1), wg, row(b_gate),
                w_o_mla[layer].astype(bf), w_o_diff[layer].astype(bf), w_o[layer].astype(bf),
                row(ln1_g), row(ln1_b), w_up[layer].astype(bf), w_down[layer].astype(bf), row(ln2_g), row(ln2_b))
    return out.reshape(b, s, d)
```

```python
import functools
import math

import jax
import jax.numpy as jnp
import numpy as np
from jax import lax
from jax.experimental import pallas as pl
from jax.experimental.pallas import tpu as pltpu

D_MODEL = 1024
MLA_HEADS = 8
MLA_NOPE_DIM = 64
MLA_ROPE_DIM = 32
MLA_V_DIM = 64
MLA_Q_RANK = 384
MLA_KV_RANK = 256
ROPE_THETA = 10000.0
DIFF_HEADS = 8
DIFF_HEAD_DIM = 64
DIFF_V_DIM = 128
D_FF = 4 * D_MODEL
DEPTH = 1
LN_EPS = 1e-5
RMS_EPS = 1e-6
NEG_INF = -1e30
DEEPNORM_ALPHA = (2.0 * DEPTH) ** 0.25
LAMBDA_INIT = 0.8 - 0.6 * math.exp(-0.3 * 0)
LOG2E = math.log2(math.e)

LANES = 128
HEAD_SLAB = LANES
VMEM_LIMIT = 56 * 1024 * 1024
PIECES = 3
F32_EXACT_INT = 1 << 24

PROJ_ROWS = 512
MLA_TILE = 512
DIFF_TILE = 256
POST_ROWS = 512
FF_CHUNK = 1024

_O1 = MLA_Q_RANK
_O2 = _O1 + MLA_KV_RANK
_O3 = _O2 + MLA_ROPE_DIM
_O4 = _O3 + DIFF_HEADS * 2 * DIFF_HEAD_DIM
_O5 = _O4 + DIFF_HEADS * 2 * DIFF_HEAD_DIM
_O6 = _O5 + DIFF_HEADS * DIFF_V_DIM
_GATE_BLOCK = 0
_DQ_BLOCK, _DK_BLOCK, _DV_BLOCK, _SMALL_BLOCK = 2, 3, 4, 5


def _alibi_slopes(n):
    def pow2_slopes(k):
        start = 2.0 ** (-8.0 / k)
        return [start ** (i + 1) for i in range(k)]
    if math.log2(n).is_integer():
        s = pow2_slopes(n)
    else:
        c = 2 ** int(math.floor(math.log2(n)))
        s = pow2_slopes(c) + pow2_slopes(2 * c)[0::2][: n - c]
    return np.asarray(s, dtype=np.float32)


def _resident(shape, col_block=0):
    index = (0,) * (len(shape) - 1) + (col_block,)
    return pl.BlockSpec(shape, lambda *_: index, pipeline_mode=pl.Buffered(1))


def _rms(x, g):
    ms = jnp.mean(x * x, axis=-1, keepdims=True)
    return x * lax.rsqrt(ms + RMS_EPS) * g


def _layer_norm(x, g, b):
    mu = jnp.mean(x, axis=-1, keepdims=True)
    xc = x - mu
    var = jnp.mean(xc * xc, axis=-1, keepdims=True)
    return xc * lax.rsqrt(var + LN_EPS) * g + b


def _dot(a, b):
    return jnp.dot(a, b, preferred_element_type=jnp.float32)


def _dot_nt(a, b):
    return lax.dot_general(a, b, (((1,), (1,)), ((), ())), preferred_element_type=jnp.float32)


def _bf16_pieces(x):
    out = []
    for _ in range(PIECES):
        p = x.astype(jnp.bfloat16).astype(jnp.float32)
        out.append(p)
        x = x - p
    return out


def _piece_pattern(pieces, by_group, axis, shape):
    idx = lax.broadcasted_iota(jnp.int32, shape, axis)
    out = jnp.zeros(shape, jnp.float32)
    for a in range(PIECES):
        for b in range(PIECES):
            out = jnp.where(idx == PIECES * a + b, pieces[a if by_group else b], out)
    return out


def _token_tables(pos_row, invf_col, scale):
    rows = pos_row.shape[1]
    half_r = MLA_ROPE_DIM // 2
    ang = invf_col * pos_row
    cos, sin = jnp.cos(ang), jnp.sin(ang)
    pad = jnp.zeros((HEAD_SLAB - MLA_NOPE_DIM - MLA_ROPE_DIM, rows), jnp.float32)
    nope0 = jnp.zeros((MLA_NOPE_DIM, rows), jnp.float32)
    slab = lambda nope, t: jnp.concatenate([nope, t, t, pad], axis=0).T
    pieces = _piece_pattern(_bf16_pieces(pos_row), False, 0, (half_r, rows))
    kpos = jnp.concatenate([pieces, jnp.zeros((LANES - half_r, rows), jnp.float32)], axis=0).T
    return (slab(nope0 + scale, cos * scale), slab(nope0, sin * scale), slab(nope0, cos), slab(nope0, sin), kpos)


def _proj_kernel(x_ref, pos_ref, invf_ref, wsmall_ref, wdq_ref, wdk_ref, wdv_ref, wuq_ref, wukv_ref,
                 gq_ref, gkv_ref, q_ref, k_ref, v_ref, dq_ref, dk_ref, dv_ref, kpos_ref):
    xb = x_ref[...].astype(jnp.bfloat16)
    scale = (MLA_NOPE_DIM + MLA_ROPE_DIM) ** -0.5 * LOG2E
    cos_s, sin_s, cos, sin, kpos = _token_tables(pos_ref[0].astype(jnp.float32), invf_ref[...], scale)
    kpos_ref[...] = kpos.astype(kpos_ref.dtype)

    small = _dot(xb, wsmall_ref[...])

    cq = _rms(small[:, :_O1], gq_ref[...]).astype(jnp.bfloat16)
    qq = _dot(cq, wuq_ref[...])
    half = MLA_HEADS * HEAD_SLAB
    for h in range(MLA_HEADS):
        lo = h * HEAD_SLAB
        q_ref[:, lo:lo + HEAD_SLAB] = (qq[:, lo:lo + HEAD_SLAB] * cos_s
                                       + qq[:, half + lo:half + lo + HEAD_SLAB] * sin_s).astype(q_ref.dtype)

    kpe = small[:, _O2:_O2 + HEAD_SLAB] * cos + small[:, _O2 + HEAD_SLAB:_O2 + 2 * HEAD_SLAB] * sin

    ckv = _rms(small[:, _O1:_O2], gkv_ref[...]).astype(jnp.bfloat16)
    kv = _dot(ckv, wukv_ref[...])
    for h in range(MLA_HEADS):
        lo = h * HEAD_SLAB
        k_ref[:, lo:lo + HEAD_SLAB] = (kv[:, lo:lo + HEAD_SLAB] + kpe).astype(k_ref.dtype)
    v_ref[...] = kv[:, half:].astype(v_ref.dtype)

    dq_ref[...] = (_dot(xb, wdq_ref[...]) * (DIFF_HEAD_DIM ** -0.5 * LOG2E)).astype(dq_ref.dtype)
    dk_ref[...] = _dot(xb, wdk_ref[...]).astype(dk_ref.dtype)
    dv_ref[...] = _dot(xb, wdv_ref[...]).astype(dv_ref.dtype)


def _proj(x2, pos_rows, invf, wcat, wuq, wukv, gq, gkv):
    t, d = x2.shape
    rows = PROJ_ROWS
    bf = jnp.bfloat16
    tok = lambda w: pl.BlockSpec((rows, w), lambda i: (i, 0))
    wblock = lambda j: _resident((d, D_MODEL), j)
    out_shape = [jax.ShapeDtypeStruct((t, MLA_HEADS * HEAD_SLAB), bf),
                 jax.ShapeDtypeStruct((t, MLA_HEADS * HEAD_SLAB), bf),
                 jax.ShapeDtypeStruct((t, MLA_HEADS * MLA_V_DIM), bf),
                 jax.ShapeDtypeStruct((t, D_MODEL), bf),
                 jax.ShapeDtypeStruct((t, D_MODEL), bf),
                 jax.ShapeDtypeStruct((t, D_MODEL), bf),
                 jax.ShapeDtypeStruct((t, LANES), bf)]
    return pl.pallas_call(
        _proj_kernel,
        grid=(t // rows,),
        in_specs=[tok(D_MODEL), pl.BlockSpec((1, 1, rows), lambda i: (i, 0, 0)), _resident(invf.shape),
                  wblock(_SMALL_BLOCK), wblock(_DQ_BLOCK), wblock(_DK_BLOCK), wblock(_DV_BLOCK),
                  _resident(wuq.shape), _resident(wukv.shape), _resident(gq.shape), _resident(gkv.shape)],
        out_specs=[tok(s.shape[1]) for s in out_shape],
        out_shape=out_shape,
        compiler_params=pltpu.CompilerParams(dimension_semantics=("parallel",), vmem_limit_bytes=VMEM_LIMIT),
        name="proj",
    )(x2, pos_rows, invf, wcat, wcat, wcat, wcat, wuq, wukv, gq, gkv)


def _causal_mask(tq, tk):
    row = lax.broadcasted_iota(jnp.int32, (tq, tk), 0)
    col = lax.broadcasted_iota(jnp.int32, (tq, tk), 1)
    return col <= row


def _tile_attention(scores, vaug_ref, r0, mask):
    t = mask.shape[1]
    pv = lambda s, m, lo, hi: _dot(jnp.exp2(s - m).astype(jnp.bfloat16), vaug_ref[lo:hi, :])
    s_d = jnp.where(mask, scores(r0, r0 + t), NEG_INF)
    m = jnp.max(s_d, axis=-1, keepdims=True)
    if r0 > 0:
        s_p = scores(0, r0)
        m = jnp.maximum(m, jnp.max(s_p, axis=-1, keepdims=True))
    acc = pv(s_d, m, r0, r0 + t)
    if r0 > 0:
        acc = acc + pv(s_p, m, 0, r0)
    return acc[:, :LANES] / acc[:, LANES:]


def _fill_values(v_ref, vaug_ref):
    vaug_ref[:, :LANES] = v_ref[0]
    vaug_ref[:, LANES:] = jnp.ones((vaug_ref.shape[0], LANES), vaug_ref.dtype)


def _mla_kernel(q_ref, k_ref, v_ref, o_ref, vaug_ref):
    tq = MLA_TILE
    seq = q_ref.shape[1]
    _fill_values(v_ref, vaug_ref)
    mask = _causal_mask(tq, tq)
    lane = lax.broadcasted_iota(jnp.int32, (tq, 2 * MLA_V_DIM), 1)
    for i in reversed(range(seq // tq)):
        r0 = i * tq
        outs = []
        for hh in range(2):
            lo = hh * HEAD_SLAB
            q = q_ref[0, r0:r0 + tq, lo:lo + HEAD_SLAB]
            scores = lambda a, b, q=q, lo=lo: _dot_nt(q, k_ref[0, a:b, lo:lo + HEAD_SLAB])
            outs.append(_tile_attention(scores, vaug_ref, r0, mask))
        o_ref[0, r0:r0 + tq, :] = jnp.where(lane < MLA_V_DIM, outs[0], outs[1]).astype(o_ref.dtype)


def _mla_attention(q, k, v):
    b, s, _ = q.shape
    pairs = MLA_HEADS // 2
    blk = lambda w: pl.BlockSpec((1, s, w), lambda bi, p: (bi, 0, p))
    return pl.pallas_call(
        _mla_kernel,
        grid=(b, pairs),
        in_specs=[blk(2 * HEAD_SLAB), blk(2 * HEAD_SLAB), blk(2 * MLA_V_DIM)],
        out_specs=blk(2 * MLA_V_DIM),
        out_shape=jax.ShapeDtypeStruct((b, s, MLA_HEADS * MLA_V_DIM), jnp.bfloat16),
        scratch_shapes=[pltpu.VMEM((s, 2 * LANES), jnp.bfloat16)],
        compiler_params=pltpu.CompilerParams(dimension_semantics=("parallel", "parallel"),
                                             vmem_limit_bytes=VMEM_LIMIT),
        name="mla_attn",
    )(q, k, v)


def _diff_kernel(slopes_ref, ok_ref, q_ref, k_ref, v_ref, kpos_ref, pk_ref, lq1_ref, lk1_ref, lq2_ref,
                 lk2_ref, g_ref, o_ref, vaug_ref, kaug_ref):
    tq = DIFF_TILE
    seq = q_ref.shape[1]
    _fill_values(v_ref, vaug_ref)
    both = lambda a: jnp.concatenate([a, a], axis=0)
    mask = both(_causal_mask(tq, tq))
    slope = slopes_ref[pl.program_id(1)] * LOG2E
    lane = lax.broadcasted_iota(jnp.int32, (tq, LANES), 1)
    lam = (jnp.exp(jnp.sum(lq1_ref[...] * lk1_ref[...], axis=-1, keepdims=True))
           - jnp.exp(jnp.sum(lq2_ref[...] * lk2_ref[...], axis=-1, keepdims=True)) + LAMBDA_INIT)

    def tiles(fast):
        if fast:
            kaug_ref[:, :LANES] = k_ref[0]
            kaug_ref[:, LANES:] = kpos_ref[0]
            coef = _piece_pattern(_bf16_pieces(jnp.full((1, 1), slope, jnp.float32)), True, 1, (1, LANES))
            coef = jnp.broadcast_to(coef.astype(jnp.bfloat16), (2 * tq, LANES))
        for i in reversed(range(seq // tq)):
            r0 = i * tq
            q = q_ref[0, r0:r0 + tq, :]
            zero = jnp.zeros_like(q)
            qq = jnp.concatenate([jnp.where(lane < DIFF_HEAD_DIM, q, zero),
                                  jnp.where(lane < DIFF_HEAD_DIM, zero, q)], axis=0)
            if fast:
                qq = jnp.concatenate([qq, coef], axis=1)
                scores = lambda a, b, qq=qq: _dot_nt(qq, kaug_ref[a:b, :])
            else:
                pq = jnp.broadcast_to(pk_ref[0, :, r0:r0 + tq], (LANES, tq)).T[:, :1]

                def scores(a, b, qq=qq, pq=pq):
                    dist = jnp.abs(pq - pk_ref[0, :, a:b]).astype(jnp.float32)
                    return _dot_nt(qq, k_ref[0, a:b, :]) - both(dist * slope)

            out = _tile_attention(scores, vaug_ref, r0, mask)
            o = out[:tq] - lam * out[tq:]
            o_ref[0, r0:r0 + tq, :] = (_rms(o, g_ref[...]) * (1.0 - LAMBDA_INIT)).astype(o_ref.dtype)

    ok = ok_ref[pl.program_id(0)]
    pl.when(ok == 1)(functools.partial(tiles, True))
    pl.when(ok != 1)(functools.partial(tiles, False))


def _diff_attention(slopes, ok, q, k, v, kpos, pos_row, lq1, lk1, lq2, lk2, g):
    b, s, _ = q.shape
    vec = lambda a: pl.BlockSpec(a.shape, lambda bi, h: (0, 0))
    head = pl.BlockSpec((1, s, LANES), lambda bi, h: (bi, 0, h))
    smem = pl.BlockSpec(memory_space=pltpu.SMEM)
    return pl.pallas_call(
        _diff_kernel,
        grid=(b, DIFF_HEADS),
        in_specs=[smem, smem, head, head, head,
                  pl.BlockSpec((1, s, LANES), lambda bi, h: (bi, 0, 0)),
                  pl.BlockSpec((1, 1, s), lambda bi, h: (bi, 0, 0)),
                  vec(lq1), vec(lk1), vec(lq2), vec(lk2), vec(g)],
        out_specs=head,
        out_shape=jax.ShapeDtypeStruct((b, s, DIFF_HEADS * DIFF_V_DIM), jnp.bfloat16),
        scratch_shapes=[pltpu.VMEM((s, 2 * LANES), jnp.bfloat16), pltpu.VMEM((s, 2 * LANES), jnp.bfloat16)],
        compiler_params=pltpu.CompilerParams(dimension_semantics=("parallel", "parallel"),
                                             vmem_limit_bytes=VMEM_LIMIT),
        name="diff_attn",
    )(slopes, ok, q, k, v, kpos, pos_row, lq1, lk1, lq2, lk2, g)


def _post_kernel(x_ref, om_ref, od_ref, wg_ref, bg_ref, wom_ref, wod_ref, wo_ref, g1_ref, b1_ref,
                 wup_ref, wdn_ref, g2_ref, b2_ref, o_ref):
    x = x_ref[...]
    xb = x.astype(jnp.bfloat16)
    gate = jax.nn.sigmoid(_dot(xb, wg_ref[...]) + bg_ref[...])
    y_mla = _dot(om_ref[...], wom_ref[...])
    y_diff = _dot(od_ref[...], wod_ref[...])
    merged = gate[:, :D_MODEL] * y_mla + gate[:, D_MODEL:] * y_diff
    mixed = _dot(merged.astype(jnp.bfloat16), wo_ref[...])
    x1 = _layer_norm(DEEPNORM_ALPHA * x + mixed, g1_ref[...], b1_ref[...])
    x1b = x1.astype(jnp.bfloat16)
    acc = DEEPNORM_ALPHA * x1
    for c in range(D_FF // FF_CHUNK):
        lo = c * FF_CHUNK
        hid = jnp.maximum(_dot(x1b, wup_ref[:, lo:lo + FF_CHUNK]), 0.0)
        acc = acc + _dot((hid * hid).astype(jnp.bfloat16), wdn_ref[lo:lo + FF_CHUNK, :])
    o_ref[...] = _layer_norm(acc, g2_ref[...], b2_ref[...]).astype(o_ref.dtype)


def _post(x2, o_mla, o_diff, wcat, bg, wom, wod, wo, g1, b1, wup, wdn, g2, b2):
    t, d = x2.shape
    rows = POST_ROWS
    tok = lambda w: pl.BlockSpec((rows, w), lambda i: (i, 0))
    consts = (bg, wom, wod, wo, g1, b1, wup, wdn, g2, b2)
    return pl.pallas_call(
        _post_kernel,
        grid=(t // rows,),
        in_specs=([tok(D_MODEL), tok(o_mla.shape[1]), tok(o_diff.shape[1]), _resident((d, 2 * D_MODEL), _GATE_BLOCK)]
                  + [_resident(a.shape) for a in consts]),
        out_specs=tok(D_MODEL),
        out_shape=jax.ShapeDtypeStruct((t, D_MODEL), jnp.float32),
        compiler_params=pltpu.CompilerParams(dimension_semantics=("parallel",), vmem_limit_bytes=VMEM_LIMIT),
        name="post",
    )(x2, o_mla, o_diff, wcat, *consts)


def _rot_half_cols(w):
    half = w.shape[-1] // 2
    return jnp.concatenate([-w[..., half:], w[..., :half]], axis=-1)


def _prep_weights(w_in, w_uq, w_ukv):
    bf = jnp.bfloat16
    f32 = jnp.float32
    pad_r = HEAD_SLAB - MLA_NOPE_DIM - MLA_ROPE_DIM
    d = w_in.shape[0]
    wkpe = w_in[:, _O2:_O3]
    slab = lambda w: jnp.concatenate([jnp.zeros((d, MLA_NOPE_DIM), f32), w, jnp.zeros((d, pad_r), f32)], axis=1)
    small = [w_in[:, :_O2], slab(wkpe), slab(_rot_half_cols(wkpe))]
    small.append(jnp.zeros((d, D_MODEL - sum(a.shape[1] for a in small)), f32))
    wcat = jnp.concatenate([w_in[:, _O6:], w_in[:, _O3:_O6]] + small, axis=1).astype(bf)

    r = w_uq.shape[0]
    uq = w_uq.reshape(r, MLA_HEADS, MLA_NOPE_DIM + MLA_ROPE_DIM)
    nope, pe = uq[..., :MLA_NOPE_DIM], uq[..., MLA_NOPE_DIM:]
    plain = jnp.concatenate([nope, pe, jnp.zeros((r, MLA_HEADS, pad_r), f32)], axis=-1)
    rot = jnp.concatenate([jnp.zeros_like(nope), _rot_half_cols(pe), jnp.zeros((r, MLA_HEADS, pad_r), f32)], axis=-1)
    wuq = jnp.concatenate([plain.reshape(r, -1), rot.reshape(r, -1)], axis=1).astype(bf)

    r = w_ukv.shape[0]
    ukv = w_ukv.reshape(r, MLA_HEADS, MLA_NOPE_DIM + MLA_V_DIM)
    kn, vv = ukv[..., :MLA_NOPE_DIM], ukv[..., MLA_NOPE_DIM:]
    kslab = jnp.concatenate([kn, jnp.zeros((r, MLA_HEADS, HEAD_SLAB - MLA_NOPE_DIM), f32)], axis=-1)
    wukv = jnp.concatenate([kslab.reshape(r, -1), vv.reshape(r, -1)], axis=1).astype(bf)
    return wcat, wuq, wukv


def kernel(x, positions, w_in, b_gate, mla_q_norm, mla_kv_norm, w_uq, w_ukv, w_o_mla, diff_lambda_q1, diff_lambda_k1, diff_lambda_q2, diff_lambda_k2, diff_subln, w_o_diff, w_o, ln1_g, ln1_b, w_up, w_down, ln2_g, ln2_b):
    b, s, d = x.shape
    bf = jnp.bfloat16
    layer = 0
    wcat, wuq, wukv = _prep_weights(w_in[layer], w_uq[layer], w_ukv[layer])

    inv_freq = 1.0 / (ROPE_THETA ** (jnp.arange(0, MLA_ROPE_DIM, 2, dtype=jnp.float32) / MLA_ROPE_DIM))
    x2 = x.reshape(b * s, d)
    q, k, v, dq, dk, dv, kpos = _proj(x2, positions.reshape(b * s // PROJ_ROWS, 1, PROJ_ROWS), inv_freq[:, None],
                                      wcat, wuq, wukv, mla_q_norm[layer][None, :], mla_kv_norm[layer][None, :])
    r3 = lambda a: a.reshape(b, s, a.shape[-1])
    o_mla = _mla_attention(r3(q), r3(k), r3(v))

    slopes = jnp.asarray(_alibi_slopes(DIFF_HEADS))
    ordered = jnp.all(positions[:, 1:] >= positions[:, :-1], axis=1)
    exact_f32 = jnp.all((positions < F32_EXACT_INT) & (positions > -F32_EXACT_INT), axis=1)
    ok = (ordered & exact_f32).astype(jnp.int32)
    o_diff = _diff_attention(slopes, ok, r3(dq), r3(dk), r3(dv), r3(kpos), positions.reshape(b, 1, s),
                             diff_lambda_q1[layer][None, :], diff_lambda_k1[layer][None, :],
                             diff_lambda_q2[layer][None, :], diff_lambda_k2[layer][None, :],
                             diff_subln[layer][None, :])

    row = lambda a: a[layer][None, :]
    out = _post(x2, o_mla.reshape(b * s, -1), o_diff.reshape(b * s, -1), wcat, row(b_gate),
                w_o_mla[layer].astype(bf), w_o_diff[layer].astype(bf), w_o[layer].astype(bf),
                row(ln1_g), row(ln1_b), w_up[layer].astype(bf), w_down[layer].astype(bf), row(ln2_g), row(ln2_b))
    return out.reshape(b, s, d)
```

```python
import functools
import math

import jax
import jax.numpy as jnp
import numpy as np
from jax import lax
from jax.experimental import pallas as pl
from jax.experimental.pallas import tpu as pltpu

D_MODEL = 1024
MLA_HEADS = 8
MLA_NOPE_DIM = 64
MLA_ROPE_DIM = 32
MLA_V_DIM = 64
MLA_Q_RANK = 384
MLA_KV_RANK = 256
ROPE_THETA = 10000.0
DIFF_HEADS = 8
DIFF_HEAD_DIM = 64
DIFF_V_DIM = 128
D_FF = 4 * D_MODEL
DEPTH = 1
LN_EPS = 1e-5
RMS_EPS = 1e-6
NEG_INF = -1e30
DEEPNORM_ALPHA = (2.0 * DEPTH) ** 0.25
LAMBDA_INIT = 0.8 - 0.6 * math.exp(-0.3 * 0)
LOG2E = math.log2(math.e)

LANES = 128
HEAD_SLAB = LANES
VMEM_LIMIT = 56 * 1024 * 1024
PIECES = 3
F32_EXACT_INT = 1 << 24

PROJ_ROWS = 512
MLA_TILE = 512
DIFF_TILE = 256
POST_ROWS = 512
FF_CHUNK = 1024

_O1 = MLA_Q_RANK
_O2 = _O1 + MLA_KV_RANK
_O3 = _O2 + MLA_ROPE_DIM
_O4 = _O3 + DIFF_HEADS * 2 * DIFF_HEAD_DIM
_O5 = _O4 + DIFF_HEADS * 2 * DIFF_HEAD_DIM
_O6 = _O5 + DIFF_HEADS * DIFF_V_DIM


def _alibi_slopes(n):
    def pow2_slopes(k):
        start = 2.0 ** (-8.0 / k)
        return [start ** (i + 1) for i in range(k)]
    if math.log2(n).is_integer():
        s = pow2_slopes(n)
    else:
        c = 2 ** int(math.floor(math.log2(n)))
        s = pow2_slopes(c) + pow2_slopes(2 * c)[0::2][: n - c]
    return np.asarray(s, dtype=np.float32)


def _resident(shape):
    return pl.BlockSpec(shape, lambda *_: (0,) * len(shape), pipeline_mode=pl.Buffered(1))


def _resident_rows(start, stop, width):
    return pl.BlockSpec((pl.Element(stop - start), pl.Element(width)), lambda *_: (start, 0),
                        pipeline_mode=pl.Buffered(1))


def _rms(x, g):
    ms = jnp.mean(x * x, axis=-1, keepdims=True)
    return x * lax.rsqrt(ms + RMS_EPS) * g


def _layer_norm(x, g, b):
    mu = jnp.mean(x, axis=-1, keepdims=True)
    xc = x - mu
    var = jnp.mean(xc * xc, axis=-1, keepdims=True)
    return xc * lax.rsqrt(var + LN_EPS) * g + b


def _dot(a, b):
    return jnp.dot(a, b, preferred_element_type=jnp.float32)


def _dot_nt(a, b):
    return lax.dot_general(a, b, (((1,), (1,)), ((), ())), preferred_element_type=jnp.float32)


def _bf16_pieces(x):
    out = []
    for _ in range(PIECES):
        p = x.astype(jnp.bfloat16).astype(jnp.float32)
        out.append(p)
        x = x - p
    return out


def _piece_pattern(pieces, by_group, axis, shape):
    idx = lax.broadcasted_iota(jnp.int32, shape, axis)
    out = jnp.zeros(shape, jnp.float32)
    for a in range(PIECES):
        for b in range(PIECES):
            out = jnp.where(idx == PIECES * a + b, pieces[a if by_group else b], out)
    return out


def _token_tables(pos_row, invf_col, scale):
    rows = pos_row.shape[1]
    half_r = MLA_ROPE_DIM // 2
    ang = invf_col * pos_row
    cos, sin = jnp.cos(ang), jnp.sin(ang)
    pad = jnp.zeros((HEAD_SLAB - MLA_NOPE_DIM - MLA_ROPE_DIM, rows), jnp.float32)
    nope0 = jnp.zeros((MLA_NOPE_DIM, rows), jnp.float32)
    slab = lambda nope, t: jnp.concatenate([nope, t, t, pad], axis=0).T
    pieces = _piece_pattern(_bf16_pieces(pos_row), False, 0, (half_r, rows))
    kpos = jnp.concatenate([pieces, jnp.zeros((LANES - half_r, rows), jnp.float32)], axis=0).T
    return (slab(nope0 + scale, cos * scale), slab(nope0, sin * scale), slab(nope0, cos), slab(nope0, sin), kpos)


def _proj_kernel(x_ref, pos_ref, invf_ref, wlat_ref, wkpe_ref, wdq_ref, wdk_ref, wdv_ref, wuq_ref, wukv_ref,
                 gq_ref, gkv_ref, q_ref, k_ref, v_ref, dq_ref, dk_ref, dv_ref, kpos_ref):
    xb = x_ref[...].astype(jnp.bfloat16)
    scale = (MLA_NOPE_DIM + MLA_ROPE_DIM) ** -0.5 * LOG2E
    cos_s, sin_s, cos, sin, kpos = _token_tables(pos_ref[0].astype(jnp.float32), invf_ref[...], scale)
    kpos_ref[...] = kpos.astype(kpos_ref.dtype)

    lat = _dot_nt(xb, wlat_ref[...])

    cq = _rms(lat[:, :_O1], gq_ref[...]).astype(jnp.bfloat16)
    qq = _dot(cq, wuq_ref[...])
    half = MLA_HEADS * HEAD_SLAB
    for h in range(MLA_HEADS):
        lo = h * HEAD_SLAB
        q_ref[:, lo:lo + HEAD_SLAB] = (qq[:, lo:lo + HEAD_SLAB] * cos_s
                                       + qq[:, half + lo:half + lo + HEAD_SLAB] * sin_s).astype(q_ref.dtype)

    kp = _dot_nt(xb, wkpe_ref[...])
    kpe = kp[:, :HEAD_SLAB] * cos + kp[:, HEAD_SLAB:] * sin

    ckv = _rms(lat[:, _O1:_O2], gkv_ref[...]).astype(jnp.bfloat16)
    kv = _dot(ckv, wukv_ref[...])
    for h in range(MLA_HEADS):
        lo = h * HEAD_SLAB
        k_ref[:, lo:lo + HEAD_SLAB] = (kv[:, lo:lo + HEAD_SLAB] + kpe).astype(k_ref.dtype)
    v_ref[...] = kv[:, half:].astype(v_ref.dtype)

    dq_ref[...] = (_dot_nt(xb, wdq_ref[...]) * (DIFF_HEAD_DIM ** -0.5 * LOG2E)).astype(dq_ref.dtype)
    dk_ref[...] = _dot_nt(xb, wdk_ref[...]).astype(dk_ref.dtype)
    dv_ref[...] = _dot_nt(xb, wdv_ref[...]).astype(dv_ref.dtype)


def _proj(x2, pos_rows, invf, w_in_t, wkpe_t, wuq, wukv, gq, gkv):
    t, d = x2.shape
    rows = PROJ_ROWS
    bf = jnp.bfloat16
    tok = lambda w: pl.BlockSpec((rows, w), lambda i: (i, 0))
    wrows = lambda a, b: _resident_rows(a, b, d)
    out_shape = [jax.ShapeDtypeStruct((t, MLA_HEADS * HEAD_SLAB), bf),
                 jax.ShapeDtypeStruct((t, MLA_HEADS * HEAD_SLAB), bf),
                 jax.ShapeDtypeStruct((t, MLA_HEADS * MLA_V_DIM), bf),
                 jax.ShapeDtypeStruct((t, D_MODEL), bf),
                 jax.ShapeDtypeStruct((t, D_MODEL), bf),
                 jax.ShapeDtypeStruct((t, D_MODEL), bf),
                 jax.ShapeDtypeStruct((t, LANES), bf)]
    return pl.pallas_call(
        _proj_kernel,
        grid=(t // rows,),
        in_specs=[tok(D_MODEL), pl.BlockSpec((1, 1, rows), lambda i: (i, 0, 0)), _resident(invf.shape),
                  wrows(0, _O2), _resident(wkpe_t.shape), wrows(_O3, _O4), wrows(_O4, _O5), wrows(_O5, _O6),
                  _resident(wuq.shape), _resident(wukv.shape), _resident(gq.shape), _resident(gkv.shape)],
        out_specs=[tok(s.shape[1]) for s in out_shape],
        out_shape=out_shape,
        compiler_params=pltpu.CompilerParams(dimension_semantics=("parallel",), vmem_limit_bytes=VMEM_LIMIT),
        name="proj",
    )(x2, pos_rows, invf, w_in_t, wkpe_t, w_in_t, w_in_t, w_in_t, wuq, wukv, gq, gkv)


def _causal_mask(tq, tk):
    row = lax.broadcasted_iota(jnp.int32, (tq, tk), 0)
    col = lax.broadcasted_iota(jnp.int32, (tq, tk), 1)
    return col <= row


def _tile_attention(scores, vaug_ref, r0, mask):
    t = mask.shape[1]
    pv = lambda s, m, lo, hi: _dot(jnp.exp2(s - m).astype(jnp.bfloat16), vaug_ref[lo:hi, :])
    s_d = jnp.where(mask, scores(r0, r0 + t), NEG_INF)
    m = jnp.max(s_d, axis=-1, keepdims=True)
    if r0 > 0:
        s_p = scores(0, r0)
        m = jnp.maximum(m, jnp.max(s_p, axis=-1, keepdims=True))
    acc = pv(s_d, m, r0, r0 + t)
    if r0 > 0:
        acc = acc + pv(s_p, m, 0, r0)
    return acc[:, :LANES] / acc[:, LANES:]


def _fill_values(v_ref, vaug_ref):
    vaug_ref[:, :LANES] = v_ref[0]
    vaug_ref[:, LANES:] = jnp.ones((vaug_ref.shape[0], LANES), vaug_ref.dtype)


def _mla_kernel(q_ref, k_ref, v_ref, o_ref, vaug_ref):
    tq = MLA_TILE
    seq = q_ref.shape[1]
    _fill_values(v_ref, vaug_ref)
    mask = _causal_mask(tq, tq)
    lane = lax.broadcasted_iota(jnp.int32, (tq, 2 * MLA_V_DIM), 1)
    for i in reversed(range(seq // tq)):
        r0 = i * tq
        outs = []
        for hh in range(2):
            lo = hh * HEAD_SLAB
            q = q_ref[0, r0:r0 + tq, lo:lo + HEAD_SLAB]
            scores = lambda a, b, q=q, lo=lo: _dot_nt(q, k_ref[0, a:b, lo:lo + HEAD_SLAB])
            outs.append(_tile_attention(scores, vaug_ref, r0, mask))
        o_ref[0, r0:r0 + tq, :] = jnp.where(lane < MLA_V_DIM, outs[0], outs[1]).astype(o_ref.dtype)


def _mla_attention(q, k, v):
    b, s, _ = q.shape
    pairs = MLA_HEADS // 2
    blk = lambda w: pl.BlockSpec((1, s, w), lambda bi, p: (bi, 0, p))
    return pl.pallas_call(
        _mla_kernel,
        grid=(b, pairs),
        in_specs=[blk(2 * HEAD_SLAB), blk(2 * HEAD_SLAB), blk(2 * MLA_V_DIM)],
        out_specs=blk(2 * MLA_V_DIM),
        out_shape=jax.ShapeDtypeStruct((b, s, MLA_HEADS * MLA_V_DIM), jnp.bfloat16),
        scratch_shapes=[pltpu.VMEM((s, 2 * LANES), jnp.bfloat16)],
        compiler_params=pltpu.CompilerParams(dimension_semantics=("parallel", "parallel"),
                                             vmem_limit_bytes=VMEM_LIMIT),
        name="mla_attn",
    )(q, k, v)


def _diff_kernel(slopes_ref, ok_ref, q_ref, k_ref, v_ref, kpos_ref, pk_ref, lq1_ref, lk1_ref, lq2_ref,
                 lk2_ref, g_ref, o_ref, vaug_ref, kaug_ref):
    tq = DIFF_TILE
    seq = q_ref.shape[1]
    _fill_values(v_ref, vaug_ref)
    both = lambda a: jnp.concatenate([a, a], axis=0)
    mask = both(_causal_mask(tq, tq))
    slope = slopes_ref[pl.program_id(1)] * LOG2E
    lane = lax.broadcasted_iota(jnp.int32, (tq, LANES), 1)
    lam = (jnp.exp(jnp.sum(lq1_ref[...] * lk1_ref[...], axis=-1, keepdims=True))
           - jnp.exp(jnp.sum(lq2_ref[...] * lk2_ref[...], axis=-1, keepdims=True)) + LAMBDA_INIT)

    def tiles(fast):
        if fast:
            kaug_ref[:, :LANES] = k_ref[0]
            kaug_ref[:, LANES:] = kpos_ref[0]
            coef = _piece_pattern(_bf16_pieces(jnp.full((1, 1), slope, jnp.float32)), True, 1, (1, LANES))
            coef = jnp.broadcast_to(coef.astype(jnp.bfloat16), (2 * tq, LANES))
        for i in reversed(range(seq // tq)):
            r0 = i * tq
            q = q_ref[0, r0:r0 + tq, :]
            zero = jnp.zeros_like(q)
            qq = jnp.concatenate([jnp.where(lane < DIFF_HEAD_DIM, q, zero),
                                  jnp.where(lane < DIFF_HEAD_DIM, zero, q)], axis=0)
            if fast:
                qq = jnp.concatenate([qq, coef], axis=1)
                scores = lambda a, b, qq=qq: _dot_nt(qq, kaug_ref[a:b, :])
            else:
                pq = jnp.broadcast_to(pk_ref[0, :, r0:r0 + tq], (LANES, tq)).T[:, :1]

                def scores(a, b, qq=qq, pq=pq):
                    dist = jnp.abs(pq - pk_ref[0, :, a:b]).astype(jnp.float32)
                    return _dot_nt(qq, k_ref[0, a:b, :]) - both(dist * slope)

            out = _tile_attention(scores, vaug_ref, r0, mask)
            o = out[:tq] - lam * out[tq:]
            o_ref[0, r0:r0 + tq, :] = (_rms(o, g_ref[...]) * (1.0 - LAMBDA_INIT)).astype(o_ref.dtype)

    ok = ok_ref[pl.program_id(0)]
    pl.when(ok == 1)(functools.partial(tiles, True))
    pl.when(ok != 1)(functools.partial(tiles, False))


def _diff_attention(slopes, ok, q, k, v, kpos, pos_row, lq1, lk1, lq2, lk2, g):
    b, s, _ = q.shape
    vec = lambda a: pl.BlockSpec(a.shape, lambda bi, h: (0, 0))
    head = pl.BlockSpec((1, s, LANES), lambda bi, h: (bi, 0, h))
    smem = pl.BlockSpec(memory_space=pltpu.SMEM)
    return pl.pallas_call(
        _diff_kernel,
        grid=(b, DIFF_HEADS),
        in_specs=[smem, smem, head, head, head,
                  pl.BlockSpec((1, s, LANES), lambda bi, h: (bi, 0, 0)),
                  pl.BlockSpec((1, 1, s), lambda bi, h: (bi, 0, 0)),
                  vec(lq1), vec(lk1), vec(lq2), vec(lk2), vec(g)],
        out_specs=head,
        out_shape=jax.ShapeDtypeStruct((b, s, DIFF_HEADS * DIFF_V_DIM), jnp.bfloat16),
        scratch_shapes=[pltpu.VMEM((s, 2 * LANES), jnp.bfloat16), pltpu.VMEM((s, 2 * LANES), jnp.bfloat16)],
        compiler_params=pltpu.CompilerParams(dimension_semantics=("parallel", "parallel"),
                                             vmem_limit_bytes=VMEM_LIMIT),
        name="diff_attn",
    )(slopes, ok, q, k, v, kpos, pos_row, lq1, lk1, lq2, lk2, g)


def _post_kernel(x_ref, om_ref, od_ref, wg_ref, bg_ref, wom_ref, wod_ref, wo_ref, g1_ref, b1_ref,
                 wup_ref, wdn_ref, g2_ref, b2_ref, o_ref):
    x = x_ref[...]
    xb = x.astype(jnp.bfloat16)
    gate = jax.nn.sigmoid(_dot_nt(xb, wg_ref[...]) + bg_ref[...])
    y_mla = _dot(om_ref[...], wom_ref[...])
    y_diff = _dot(od_ref[...], wod_ref[...])
    merged = gate[:, :D_MODEL] * y_mla + gate[:, D_MODEL:] * y_diff
    mixed = _dot(merged.astype(jnp.bfloat16), wo_ref[...])
    x1 = _layer_norm(DEEPNORM_ALPHA * x + mixed, g1_ref[...], b1_ref[...])
    x1b = x1.astype(jnp.bfloat16)
    acc = DEEPNORM_ALPHA * x1
    for c in range(D_FF // FF_CHUNK):
        lo = c * FF_CHUNK
        hid = jnp.maximum(_dot(x1b, wup_ref[:, lo:lo + FF_CHUNK]), 0.0)
        acc = acc + _dot((hid * hid).astype(jnp.bfloat16), wdn_ref[lo:lo + FF_CHUNK, :])
    o_ref[...] = _layer_norm(acc, g2_ref[...], b2_ref[...]).astype(o_ref.dtype)


def _post(x2, o_mla, o_diff, w_in_t, bg, wom, wod, wo, g1, b1, wup, wdn, g2, b2):
    t, d = x2.shape
    rows = POST_ROWS
    tok = lambda w: pl.BlockSpec((rows, w), lambda i: (i, 0))
    consts = (bg, wom, wod, wo, g1, b1, wup, wdn, g2, b2)
    return pl.pallas_call(
        _post_kernel,
        grid=(t // rows,),
        in_specs=([tok(D_MODEL), tok(o_mla.shape[1]), tok(o_diff.shape[1]), _resident_rows(_O6, w_in_t.shape[0], d)]
                  + [_resident(a.shape) for a in consts]),
        out_specs=tok(D_MODEL),
        out_shape=jax.ShapeDtypeStruct((t, D_MODEL), jnp.float32),
        compiler_params=pltpu.CompilerParams(dimension_semantics=("parallel",), vmem_limit_bytes=VMEM_LIMIT),
        name="post",
    )(x2, o_mla, o_diff, w_in_t, *consts)


def _rot_half_cols(w):
    half = w.shape[-1] // 2
    return jnp.concatenate([-w[..., half:], w[..., :half]], axis=-1)


def _prep_weights(w_in, w_uq, w_ukv):
    bf = jnp.bfloat16
    f32 = jnp.float32
    pad_r = HEAD_SLAB - MLA_NOPE_DIM - MLA_ROPE_DIM
    d = w_in.shape[0]
    w_in_t = jnp.swapaxes(w_in, 0, 1).astype(bf)
    wkpe = w_in[:, _O2:_O3]
    slab = lambda w: jnp.concatenate([jnp.zeros((d, MLA_NOPE_DIM), f32), w, jnp.zeros((d, pad_r), f32)], axis=1)
    wkpe_t = jnp.concatenate([slab(wkpe), slab(_rot_half_cols(wkpe))], axis=1).T.astype(bf)

    r = w_uq.shape[0]
    uq = w_uq.reshape(r, MLA_HEADS, MLA_NOPE_DIM + MLA_ROPE_DIM)
    nope, pe = uq[..., :MLA_NOPE_DIM], uq[..., MLA_NOPE_DIM:]
    plain = jnp.concatenate([nope, pe, jnp.zeros((r, MLA_HEADS, pad_r), f32)], axis=-1)
    rot = jnp.concatenate([jnp.zeros_like(nope), _rot_half_cols(pe), jnp.zeros((r, MLA_HEADS, pad_r), f32)], axis=-1)
    wuq = jnp.concatenate([plain.reshape(r, -1), rot.reshape(r, -1)], axis=1).astype(bf)

    r = w_ukv.shape[0]
    ukv = w_ukv.reshape(r, MLA_HEADS, MLA_NOPE_DIM + MLA_V_DIM)
    kn, vv = ukv[..., :MLA_NOPE_DIM], ukv[..., MLA_NOPE_DIM:]
    kslab = jnp.concatenate([kn, jnp.zeros((r, MLA_HEADS, HEAD_SLAB - MLA_NOPE_DIM), f32)], axis=-1)
    wukv = jnp.concatenate([kslab.reshape(r, -1), vv.reshape(r, -1)], axis=1).astype(bf)
    return w_in_t, wkpe_t, wuq, wukv


def kernel(x, positions, w_in, b_gate, mla_q_norm, mla_kv_norm, w_uq, w_ukv, w_o_mla, diff_lambda_q1, diff_lambda_k1, diff_lambda_q2, diff_lambda_k2, diff_subln, w_o_diff, w_o, ln1_g, ln1_b, w_up, w_down, ln2_g, ln2_b):
    b, s, d = x.shape
    bf = jnp.bfloat16
    layer = 0
    w_in_t, wkpe_t, wuq, wukv = _prep_weights(w_in[layer], w_uq[layer], w_ukv[layer])

    inv_freq = 1.0 / (ROPE_THETA ** (jnp.arange(0, MLA_ROPE_DIM, 2, dtype=jnp.float32) / MLA_ROPE_DIM))
    x2 = x.reshape(b * s, d)
    q, k, v, dq, dk, dv, kpos = _proj(x2, positions.reshape(b * s // PROJ_ROWS, 1, PROJ_ROWS), inv_freq[:, None],
                                      w_in_t, wkpe_t, wuq, wukv, mla_q_norm[layer][None, :],
                                      mla_kv_norm[layer][None, :])
    r3 = lambda a: a.reshape(b, s, a.shape[-1])
    o_mla = _mla_attention(r3(q), r3(k), r3(v))

    slopes = jnp.asarray(_alibi_slopes(DIFF_HEADS))
    ordered = jnp.all(positions[:, 1:] >= positions[:, :-1], axis=1)
    exact_f32 = jnp.all((positions < F32_EXACT_INT) & (positions > -F32_EXACT_INT), axis=1)
    ok = (ordered & exact_f32).astype(jnp.int32)
    o_diff = _diff_attention(slopes, ok, r3(dq), r3(dk), r3(dv), r3(kpos), positions.reshape(b, 1, s),
                             diff_lambda_q1[layer][None, :], diff_lambda_k1[layer][None, :],
                             diff_lambda_q2[layer][None, :], diff_lambda_k2[layer][None, :],
                             diff_subln[layer][None, :])

    row = lambda a: a[layer][None, :]
    out = _post(x2, o_mla.reshape(b * s, -1), o_diff.reshape(b * s, -1), w_in_t, row(b_gate),
                w_o_mla[layer].astype(bf), w_o_diff[layer].astype(bf), w_o[layer].astype(bf),
                row(ln1_g), row(ln1_b), w_up[layer].astype(bf), w_down[layer].astype(bf), row(ln2_g), row(ln2_b))
    return out.reshape(b, s, d)
```

```python
import functools
import math

import jax
import jax.numpy as jnp
import numpy as np
from jax import lax
from jax.experimental import pallas as pl
from jax.experimental.pallas import tpu as pltpu

D_MODEL = 1024
MLA_HEADS = 8
MLA_NOPE_DIM = 64
MLA_ROPE_DIM = 32
MLA_V_DIM = 64
MLA_Q_RANK = 384
MLA_KV_RANK = 256
ROPE_THETA = 10000.0
DIFF_HEADS = 8
DIFF_HEAD_DIM = 64
DIFF_V_DIM = 128
D_FF = 4 * D_MODEL
DEPTH = 1
LN_EPS = 1e-5
RMS_EPS = 1e-6
NEG_INF = -1e30
DEEPNORM_ALPHA = (2.0 * DEPTH) ** 0.25
LAMBDA_INIT = 0.8 - 0.6 * math.exp(-0.3 * 0)
LOG2E = math.log2(math.e)

LANES = 128
HEAD_SLAB = LANES
VMEM_LIMIT = 56 * 1024 * 1024
PIECES = 3
F32_EXACT_INT = 1 << 24

PROJ_ROWS = 512
MLA_TILE = 512
DIFF_TILE = 256
POST_ROWS = 512
POST_GROUPS = 2
FF_CHUNK = 1024

_O1 = MLA_Q_RANK
_O2 = _O1 + MLA_KV_RANK
_O3 = _O2 + MLA_ROPE_DIM
_O4 = _O3 + DIFF_HEADS * 2 * DIFF_HEAD_DIM
_O5 = _O4 + DIFF_HEADS * 2 * DIFF_HEAD_DIM
_O6 = _O5 + DIFF_HEADS * DIFF_V_DIM


def _alibi_slopes(n):
    def pow2_slopes(k):
        start = 2.0 ** (-8.0 / k)
        return [start ** (i + 1) for i in range(k)]
    if math.log2(n).is_integer():
        s = pow2_slopes(n)
    else:
        c = 2 ** int(math.floor(math.log2(n)))
        s = pow2_slopes(c) + pow2_slopes(2 * c)[0::2][: n - c]
    return np.asarray(s, dtype=np.float32)


def _resident(shape):
    return pl.BlockSpec(shape, lambda *_: (0,) * len(shape), pipeline_mode=pl.Buffered(1))


def _resident_rows(start, stop, width):
    return pl.BlockSpec((pl.Element(stop - start), pl.Element(width)), lambda *_: (start, 0),
                        pipeline_mode=pl.Buffered(1))


def _rms(x, g):
    ms = jnp.mean(x * x, axis=-1, keepdims=True)
    return x * lax.rsqrt(ms + RMS_EPS) * g


def _layer_norm(x, g, b):
    mu = jnp.mean(x, axis=-1, keepdims=True)
    xc = x - mu
    var = jnp.mean(xc * xc, axis=-1, keepdims=True)
    return xc * lax.rsqrt(var + LN_EPS) * g + b


def _dot(a, b):
    return jnp.dot(a, b, preferred_element_type=jnp.float32)


def _dot_nt(a, b):
    return lax.dot_general(a, b, (((1,), (1,)), ((), ())), preferred_element_type=jnp.float32)


def _bf16_pieces(x):
    out = []
    for _ in range(PIECES):
        p = x.astype(jnp.bfloat16).astype(jnp.float32)
        out.append(p)
        x = x - p
    return out


def _piece_pattern(pieces, by_group, axis, shape):
    idx = lax.broadcasted_iota(jnp.int32, shape, axis)
    out = jnp.zeros(shape, jnp.float32)
    for a in range(PIECES):
        for b in range(PIECES):
            out = jnp.where(idx == PIECES * a + b, pieces[a if by_group else b], out)
    return out


def _token_tables(pos_row, invf_col, scale):
    rows = pos_row.shape[1]
    half_r = MLA_ROPE_DIM // 2
    ang = invf_col * pos_row
    cos, sin = jnp.cos(ang), jnp.sin(ang)
    pad = jnp.zeros((HEAD_SLAB - MLA_NOPE_DIM - MLA_ROPE_DIM, rows), jnp.float32)
    nope0 = jnp.zeros((MLA_NOPE_DIM, rows), jnp.float32)
    slab = lambda nope, t: jnp.concatenate([nope, t, t, pad], axis=0).T
    pieces = _piece_pattern(_bf16_pieces(pos_row), False, 0, (half_r, rows))
    kpos = jnp.concatenate([pieces, jnp.zeros((LANES - half_r, rows), jnp.float32)], axis=0).T
    return (slab(nope0 + scale, cos * scale), slab(nope0, sin * scale), slab(nope0, cos), slab(nope0, sin), kpos)


def _proj_kernel(x_ref, pos_ref, invf_ref, wlat_ref, wkpe_ref, wdq_ref, wdk_ref, wdv_ref, wuq_ref, wukv_ref,
                 gq_ref, gkv_ref, q_ref, k_ref, v_ref, dq_ref, dk_ref, dv_ref, kpos_ref):
    xb = x_ref[...].astype(jnp.bfloat16)
    scale = (MLA_NOPE_DIM + MLA_ROPE_DIM) ** -0.5 * LOG2E
    cos_s, sin_s, cos, sin, kpos = _token_tables(pos_ref[0].astype(jnp.float32), invf_ref[...], scale)
    kpos_ref[...] = kpos.astype(kpos_ref.dtype)

    lat = _dot_nt(xb, wlat_ref[...])

    cq = _rms(lat[:, :_O1], gq_ref[...]).astype(jnp.bfloat16)
    qq = _dot(cq, wuq_ref[...])
    half = MLA_HEADS * HEAD_SLAB
    per_group = LANES // MLA_ROPE_DIM
    for h in range(MLA_HEADS):
        lo = h * HEAD_SLAB
        grp = half + (h // per_group) * LANES
        rot = pltpu.roll(qq[:, grp:grp + LANES], (MLA_NOPE_DIM - (h % per_group) * MLA_ROPE_DIM) % LANES, 1)
        q_ref[:, lo:lo + HEAD_SLAB] = (qq[:, lo:lo + HEAD_SLAB] * cos_s + rot * sin_s).astype(q_ref.dtype)

    kp = _dot_nt(xb, wkpe_ref[...])
    kpe = kp[:, :HEAD_SLAB] * cos + kp[:, HEAD_SLAB:] * sin

    ckv = _rms(lat[:, _O1:_O2], gkv_ref[...]).astype(jnp.bfloat16)
    kv = _dot(ckv, wukv_ref[...])
    for h in range(MLA_HEADS):
        lo = h * HEAD_SLAB
        k_ref[:, lo:lo + HEAD_SLAB] = (kv[:, lo:lo + HEAD_SLAB] + kpe).astype(k_ref.dtype)
    v_ref[...] = kv[:, half:].astype(v_ref.dtype)

    dq_ref[...] = (_dot_nt(xb, wdq_ref[...]) * (DIFF_HEAD_DIM ** -0.5 * LOG2E)).astype(dq_ref.dtype)
    dk_ref[...] = _dot_nt(xb, wdk_ref[...]).astype(dk_ref.dtype)
    dv_ref[...] = _dot_nt(xb, wdv_ref[...]).astype(dv_ref.dtype)


def _proj(x2, pos_rows, invf, w_in_t, wkpe_t, wuq, wukv, gq, gkv):
    t, d = x2.shape
    rows = PROJ_ROWS
    bf = jnp.bfloat16
    tok = lambda w: pl.BlockSpec((rows, w), lambda i: (i, 0))
    wrows = lambda a, b: _resident_rows(a, b, d)
    out_shape = [jax.ShapeDtypeStruct((t, MLA_HEADS * HEAD_SLAB), bf),
                 jax.ShapeDtypeStruct((t, MLA_HEADS * HEAD_SLAB), bf),
                 jax.ShapeDtypeStruct((t, MLA_HEADS * MLA_V_DIM), bf),
                 jax.ShapeDtypeStruct((t, D_MODEL), bf),
                 jax.ShapeDtypeStruct((t, D_MODEL), bf),
                 jax.ShapeDtypeStruct((t, D_MODEL), bf),
                 jax.ShapeDtypeStruct((t, LANES), bf)]
    return pl.pallas_call(
        _proj_kernel,
        grid=(t // rows,),
        in_specs=[tok(D_MODEL), pl.BlockSpec((1, 1, rows), lambda i: (i, 0, 0)), _resident(invf.shape),
                  wrows(0, _O2), _resident(wkpe_t.shape), wrows(_O3, _O4), wrows(_O4, _O5), wrows(_O5, _O6),
                  _resident(wuq.shape), _resident(wukv.shape), _resident(gq.shape), _resident(gkv.shape)],
        out_specs=[tok(s.shape[1]) for s in out_shape],
        out_shape=out_shape,
        compiler_params=pltpu.CompilerParams(dimension_semantics=("parallel",), vmem_limit_bytes=VMEM_LIMIT),
        name="proj",
    )(x2, pos_rows, invf, w_in_t, wkpe_t, w_in_t, w_in_t, w_in_t, wuq, wukv, gq, gkv)


def _causal_mask(tq, tk):
    row = lax.broadcasted_iota(jnp.int32, (tq, tk), 0)
    col = lax.broadcasted_iota(jnp.int32, (tq, tk), 1)
    return col <= row


def _tile_attention(scores, vaug_ref, r0, mask):
    t = mask.shape[1]
    pv = lambda s, m, lo, hi: _dot(jnp.exp2(s - m).astype(jnp.bfloat16), vaug_ref[lo:hi, :])
    s_d = jnp.where(mask, scores(r0, r0 + t), NEG_INF)
    m = jnp.max(s_d, axis=-1, keepdims=True)
    if r0 > 0:
        s_p = scores(0, r0)
        m = jnp.maximum(m, jnp.max(s_p, axis=-1, keepdims=True))
    acc = pv(s_d, m, r0, r0 + t)
    if r0 > 0:
        acc = acc + pv(s_p, m, 0, r0)
    return acc[:, :LANES] / acc[:, LANES:]


def _fill_values(v_ref, vaug_ref):
    vaug_ref[:, :LANES] = v_ref[0]
    vaug_ref[:, LANES:] = jnp.ones((vaug_ref.shape[0], LANES), vaug_ref.dtype)


def _mla_kernel(q_ref, k_ref, v_ref, o_ref, vaug_ref):
    tq = MLA_TILE
    seq = q_ref.shape[1]
    _fill_values(v_ref, vaug_ref)
    mask = _causal_mask(tq, tq)
    lane = lax.broadcasted_iota(jnp.int32, (tq, 2 * MLA_V_DIM), 1)
    for i in reversed(range(seq // tq)):
        r0 = i * tq
        outs = []
        for hh in range(2):
            lo = hh * HEAD_SLAB
            q = q_ref[0, r0:r0 + tq, lo:lo + HEAD_SLAB]
            scores = lambda a, b, q=q, lo=lo: _dot_nt(q, k_ref[0, a:b, lo:lo + HEAD_SLAB])
            outs.append(_tile_attention(scores, vaug_ref, r0, mask))
        o_ref[0, r0:r0 + tq, :] = jnp.where(lane < MLA_V_DIM, outs[0], outs[1]).astype(o_ref.dtype)


def _mla_attention(q, k, v):
    b, s, _ = q.shape
    pairs = MLA_HEADS // 2
    blk = lambda w: pl.BlockSpec((1, s, w), lambda bi, p: (bi, 0, p))
    return pl.pallas_call(
        _mla_kernel,
        grid=(b, pairs),
        in_specs=[blk(2 * HEAD_SLAB), blk(2 * HEAD_SLAB), blk(2 * MLA_V_DIM)],
        out_specs=blk(2 * MLA_V_DIM),
        out_shape=jax.ShapeDtypeStruct((b, s, MLA_HEADS * MLA_V_DIM), jnp.bfloat16),
        scratch_shapes=[pltpu.VMEM((s, 2 * LANES), jnp.bfloat16)],
        compiler_params=pltpu.CompilerParams(dimension_semantics=("parallel", "parallel"),
                                             vmem_limit_bytes=VMEM_LIMIT),
        name="mla_attn",
    )(q, k, v)


def _diff_kernel(slopes_ref, ok_ref, q_ref, k_ref, v_ref, kpos_ref, pk_ref, lq1_ref, lk1_ref, lq2_ref,
                 lk2_ref, g_ref, o_ref, vaug_ref, kaug_ref):
    tq = DIFF_TILE
    seq = q_ref.shape[1]
    _fill_values(v_ref, vaug_ref)
    both = lambda a: jnp.concatenate([a, a], axis=0)
    mask = both(_causal_mask(tq, tq))
    slope = slopes_ref[pl.program_id(1)] * LOG2E
    lane = lax.broadcasted_iota(jnp.int32, (tq, LANES), 1)
    lam = (jnp.exp(jnp.sum(lq1_ref[...] * lk1_ref[...], axis=-1, keepdims=True))
           - jnp.exp(jnp.sum(lq2_ref[...] * lk2_ref[...], axis=-1, keepdims=True)) + LAMBDA_INIT)

    def tiles(fast):
        if fast:
            kaug_ref[:, :LANES] = k_ref[0]
            kaug_ref[:, LANES:] = kpos_ref[0]
            coef = _piece_pattern(_bf16_pieces(jnp.full((1, 1), slope, jnp.float32)), True, 1, (1, LANES))
            coef = jnp.broadcast_to(coef.astype(jnp.bfloat16), (2 * tq, LANES))
        for i in reversed(range(seq // tq)):
            r0 = i * tq
            q = q_ref[0, r0:r0 + tq, :]
            zero = jnp.zeros_like(q)
            qq = jnp.concatenate([jnp.where(lane < DIFF_HEAD_DIM, q, zero),
                                  jnp.where(lane < DIFF_HEAD_DIM, zero, q)], axis=0)
            if fast:
                qq = jnp.concatenate([qq, coef], axis=1)
                scores = lambda a, b, qq=qq: _dot_nt(qq, kaug_ref[a:b, :])
            else:
                pq = jnp.broadcast_to(pk_ref[0, :, r0:r0 + tq], (LANES, tq)).T[:, :1]

                def scores(a, b, qq=qq, pq=pq):
                    dist = jnp.abs(pq - pk_ref[0, :, a:b]).astype(jnp.float32)
                    return _dot_nt(qq, k_ref[0, a:b, :]) - both(dist * slope)

            out = _tile_attention(scores, vaug_ref, r0, mask)
            o = out[:tq] - lam * out[tq:]
            o_ref[0, r0:r0 + tq, :] = (_rms(o, g_ref[...]) * (1.0 - LAMBDA_INIT)).astype(o_ref.dtype)

    ok = ok_ref[pl.program_id(0)]
    pl.when(ok == 1)(functools.partial(tiles, True))
    pl.when(ok != 1)(functools.partial(tiles, False))


def _diff_attention(slopes, ok, q, k, v, kpos, pos_row, lq1, lk1, lq2, lk2, g):
    b, s, _ = q.shape
    vec = lambda a: pl.BlockSpec(a.shape, lambda bi, h: (0, 0))
    head = pl.BlockSpec((1, s, LANES), lambda bi, h: (bi, 0, h))
    smem = pl.BlockSpec(memory_space=pltpu.SMEM)
    return pl.pallas_call(
        _diff_kernel,
        grid=(b, DIFF_HEADS),
        in_specs=[smem, smem, head, head, head,
                  pl.BlockSpec((1, s, LANES), lambda bi, h: (bi, 0, 0)),
                  pl.BlockSpec((1, 1, s), lambda bi, h: (bi, 0, 0)),
                  vec(lq1), vec(lk1), vec(lq2), vec(lk2), vec(g)],
        out_specs=head,
        out_shape=jax.ShapeDtypeStruct((b, s, DIFF_HEADS * DIFF_V_DIM), jnp.bfloat16),
        scratch_shapes=[pltpu.VMEM((s, 2 * LANES), jnp.bfloat16), pltpu.VMEM((s, 2 * LANES), jnp.bfloat16)],
        compiler_params=pltpu.CompilerParams(dimension_semantics=("parallel", "parallel"),
                                             vmem_limit_bytes=VMEM_LIMIT),
        name="diff_attn",
    )(slopes, ok, q, k, v, kpos, pos_row, lq1, lk1, lq2, lk2, g)


def _post_kernel(x_ref, om_ref, od_ref, wg_ref, bg_ref, wom_ref, wod_ref, wo_ref, g1_ref, b1_ref,
                 wup_ref, wdn_ref, g2_ref, b2_ref, o_ref):
    def mix(rows):
        x = x_ref[rows, :]
        xb = x.astype(jnp.bfloat16)
        gate = jax.nn.sigmoid(_dot_nt(xb, wg_ref[...]) + bg_ref[...])
        y_mla = _dot(om_ref[rows, :], wom_ref[...])
        y_diff = _dot(od_ref[rows, :], wod_ref[...])
        merged = gate[:, :D_MODEL] * y_mla + gate[:, D_MODEL:] * y_diff
        mixed = _dot(merged.astype(jnp.bfloat16), wo_ref[...])
        return _layer_norm(DEEPNORM_ALPHA * x + mixed, g1_ref[...], b1_ref[...])

    def mlp(rows, x1):
        x1b = x1.astype(jnp.bfloat16)
        acc = DEEPNORM_ALPHA * x1
        for c in range(D_FF // FF_CHUNK):
            lo = c * FF_CHUNK
            hid = jnp.maximum(_dot(x1b, wup_ref[:, lo:lo + FF_CHUNK]), 0.0)
            acc = acc + _dot((hid * hid).astype(jnp.bfloat16), wdn_ref[lo:lo + FF_CHUNK, :])
        o_ref[rows, :] = _layer_norm(acc, g2_ref[...], b2_ref[...]).astype(o_ref.dtype)

    sub = x_ref.shape[0] // POST_GROUPS
    groups = [slice(g * sub, (g + 1) * sub) for g in range(POST_GROUPS)]
    x1s = [mix(rows) for rows in groups]
    for rows, x1 in zip(groups, x1s):
        mlp(rows, x1)


def _post(x2, o_mla, o_diff, w_in_t, bg, wom, wod, wo, g1, b1, wup, wdn, g2, b2):
    t, d = x2.shape
    rows = POST_ROWS
    tok = lambda w: pl.BlockSpec((rows, w), lambda i: (i, 0))
    consts = (bg, wom, wod, wo, g1, b1, wup, wdn, g2, b2)
    return pl.pallas_call(
        _post_kernel,
        grid=(t // rows,),
        in_specs=([tok(D_MODEL), tok(o_mla.shape[1]), tok(o_diff.shape[1]), _resident_rows(_O6, w_in_t.shape[0], d)]
                  + [_resident(a.shape) for a in consts]),
        out_specs=tok(D_MODEL),
        out_shape=jax.ShapeDtypeStruct((t, D_MODEL), jnp.float32),
        compiler_params=pltpu.CompilerParams(dimension_semantics=("parallel",), vmem_limit_bytes=VMEM_LIMIT),
        name="post",
    )(x2, o_mla, o_diff, w_in_t, *consts)


def _rot_half_cols(w):
    half = w.shape[-1] // 2
    return jnp.concatenate([-w[..., half:], w[..., :half]], axis=-1)


def _prep_weights(w_in, w_uq, w_ukv):
    bf = jnp.bfloat16
    f32 = jnp.float32
    pad_r = HEAD_SLAB - MLA_NOPE_DIM - MLA_ROPE_DIM
    d = w_in.shape[0]
    w_in_t = jnp.swapaxes(w_in, 0, 1).astype(bf)
    wkpe = w_in[:, _O2:_O3]
    slab = lambda w: jnp.concatenate([jnp.zeros((d, MLA_NOPE_DIM), f32), w, jnp.zeros((d, pad_r), f32)], axis=1)
    wkpe_t = jnp.concatenate([slab(wkpe), slab(_rot_half_cols(wkpe))], axis=1).T.astype(bf)

    r = w_uq.shape[0]
    uq = w_uq.reshape(r, MLA_HEADS, MLA_NOPE_DIM + MLA_ROPE_DIM)
    nope, pe = uq[..., :MLA_NOPE_DIM], uq[..., MLA_NOPE_DIM:]
    plain = jnp.concatenate([nope, pe, jnp.zeros((r, MLA_HEADS, pad_r), f32)], axis=-1)
    wuq = jnp.concatenate([plain.reshape(r, -1), _rot_half_cols(pe).reshape(r, -1)], axis=1).astype(bf)

    r = w_ukv.shape[0]
    ukv = w_ukv.reshape(r, MLA_HEADS, MLA_NOPE_DIM + MLA_V_DIM)
    kn, vv = ukv[..., :MLA_NOPE_DIM], ukv[..., MLA_NOPE_DIM:]
    kslab = jnp.concatenate([kn, jnp.zeros((r, MLA_HEADS, HEAD_SLAB - MLA_NOPE_DIM), f32)], axis=-1)
    wukv = jnp.concatenate([kslab.reshape(r, -1), vv.reshape(r, -1)], axis=1).astype(bf)
    return w_in_t, wkpe_t, wuq, wukv


def kernel(x, positions, w_in, b_gate, mla_q_norm, mla_kv_norm, w_uq, w_ukv, w_o_mla, diff_lambda_q1, diff_lambda_k1, diff_lambda_q2, diff_lambda_k2, diff_subln, w_o_diff, w_o, ln1_g, ln1_b, w_up, w_down, ln2_g, ln2_b):
    b, s, d = x.shape
    bf = jnp.bfloat16
    layer = 0
    w_in_t, wkpe_t, wuq, wukv = _prep_weights(w_in[layer], w_uq[layer], w_ukv[layer])

    inv_freq = 1.0 / (ROPE_THETA ** (jnp.arange(0, MLA_ROPE_DIM, 2, dtype=jnp.float32) / MLA_ROPE_DIM))
    x2 = x.reshape(b * s, d)
    q, k, v, dq, dk, dv, kpos = _proj(x2, positions.reshape(b * s // PROJ_ROWS, 1, PROJ_ROWS), inv_freq[:, None],
                                      w_in_t, wkpe_t, wuq, wukv, mla_q_norm[layer][None, :],
                                      mla_kv_norm[layer][None, :])
    r3 = lambda a: a.reshape(b, s, a.shape[-1])
    o_mla = _mla_attention(r3(q), r3(k), r3(v))

    slopes = jnp.asarray(_alibi_slopes(DIFF_HEADS))
    ordered = jnp.all(positions[:, 1:] >= positions[:, :-1], axis=1)
    exact_f32 = jnp.all((positions < F32_EXACT_INT) & (positions > -F32_EXACT_INT), axis=1)
    ok = (ordered & exact_f32).astype(jnp.int32)
    o_diff = _diff_attention(slopes, ok, r3(dq), r3(dk), r3(dv), r3(kpos), positions.reshape(b, 1, s),
                             diff_lambda_q1[layer][None, :], diff_lambda_k1[layer][None, :],
                             diff_lambda_q2[layer][None, :], diff_lambda_k2[layer][None, :],
                             diff_subln[layer][None, :])

    row = lambda a: a[layer][None, :]
    out = _post(x2, o_mla.reshape(b * s, -1), o_diff.reshape(b * s, -1), w_in_t, row(b_gate),
                w_o_mla[layer].astype(bf), w_o_diff[layer].astype(bf), w_o[layer].astype(bf),
                row(ln1_g), row(ln1_b), w_up[layer].astype(bf), w_down[layer].astype(bf), row(ln2_g), row(ln2_b))
    return out.reshape(b, s, d)
```

```python
import functools
import math

import jax
import jax.numpy as jnp
import numpy as np
from jax import lax
from jax.experimental import pallas as pl
from jax.experimental.pallas import tpu as pltpu

D_MODEL = 1024
MLA_HEADS = 8
MLA_NOPE_DIM = 64
MLA_ROPE_DIM = 32
MLA_V_DIM = 64
MLA_Q_RANK = 384
MLA_KV_RANK = 256
ROPE_THETA = 10000.0
DIFF_HEADS = 8
DIFF_HEAD_DIM = 64
DIFF_V_DIM = 128
D_FF = 4 * D_MODEL
DEPTH = 1
LN_EPS = 1e-5
RMS_EPS = 1e-6
NEG_INF = -1e30
DEEPNORM_ALPHA = (2.0 * DEPTH) ** 0.25
LAMBDA_INIT = 0.8 - 0.6 * math.exp(-0.3 * 0)
LOG2E = math.log2(math.e)

LANES = 128
HEAD_SLAB = LANES
VMEM_LIMIT = 56 * 1024 * 1024
PIECES = 3
F32_EXACT_INT = 1 << 24

PROJ_ROWS = 512
MLA_TILE = 512
MLA_PAIRS = 4
DIFF_GROUP = 1
DIFF_TILE = 256
POST_ROWS = 512
POST_GROUPS = 2
FF_CHUNK = 1024

_O1 = MLA_Q_RANK
_O2 = _O1 + MLA_KV_RANK
_O3 = _O2 + MLA_ROPE_DIM
_O4 = _O3 + DIFF_HEADS * 2 * DIFF_HEAD_DIM
_O5 = _O4 + DIFF_HEADS * 2 * DIFF_HEAD_DIM
_O6 = _O5 + DIFF_HEADS * DIFF_V_DIM


def _alibi_slopes(n):
    def pow2_slopes(k):
        start = 2.0 ** (-8.0 / k)
        return [start ** (i + 1) for i in range(k)]
    if math.log2(n).is_integer():
        s = pow2_slopes(n)
    else:
        c = 2 ** int(math.floor(math.log2(n)))
        s = pow2_slopes(c) + pow2_slopes(2 * c)[0::2][: n - c]
    return np.asarray(s, dtype=np.float32)


def _resident(shape):
    return pl.BlockSpec(shape, lambda *_: (0,) * len(shape), pipeline_mode=pl.Buffered(1))


def _resident_rows(start, stop, width):
    return pl.BlockSpec((pl.Element(stop - start), pl.Element(width)), lambda *_: (start, 0),
                        pipeline_mode=pl.Buffered(1))


def _rms(x, g):
    ms = jnp.mean(x * x, axis=-1, keepdims=True)
    return x * lax.rsqrt(ms + RMS_EPS) * g


def _layer_norm(x, g, b):
    mu = jnp.mean(x, axis=-1, keepdims=True)
    xc = x - mu
    var = jnp.mean(xc * xc, axis=-1, keepdims=True)
    return xc * lax.rsqrt(var + LN_EPS) * g + b


def _dot(a, b):
    return jnp.dot(a, b, preferred_element_type=jnp.float32)


def _dot_nt(a, b):
    return lax.dot_general(a, b, (((1,), (1,)), ((), ())), preferred_element_type=jnp.float32)


def _bf16_pieces(x):
    out = []
    for _ in range(PIECES):
        p = x.astype(jnp.bfloat16).astype(jnp.float32)
        out.append(p)
        x = x - p
    return out


def _piece_pattern(pieces, by_group, axis, shape):
    idx = lax.broadcasted_iota(jnp.int32, shape, axis)
    out = jnp.zeros(shape, jnp.float32)
    for a in range(PIECES):
        for b in range(PIECES):
            out = jnp.where(idx == PIECES * a + b, pieces[a if by_group else b], out)
    return out


def _token_tables(pos_row, invf_col, scale):
    rows = pos_row.shape[1]
    half_r = MLA_ROPE_DIM // 2
    ang = invf_col * pos_row
    cos, sin = jnp.cos(ang), jnp.sin(ang)
    pad = jnp.zeros((HEAD_SLAB - MLA_NOPE_DIM - MLA_ROPE_DIM, rows), jnp.float32)
    nope0 = jnp.zeros((MLA_NOPE_DIM, rows), jnp.float32)
    slab = lambda nope, t: jnp.concatenate([nope, t, t, pad], axis=0).T
    pieces = _piece_pattern(_bf16_pieces(pos_row), False, 0, (half_r, rows))
    kpos = jnp.concatenate([pieces, jnp.zeros((LANES - half_r, rows), jnp.float32)], axis=0).T
    return (slab(nope0 + scale, cos * scale), slab(nope0, sin * scale), slab(nope0, cos), slab(nope0, sin), kpos)


def _proj_kernel(x_ref, pos_ref, invf_ref, wlat_ref, wkpe_ref, wdq_ref, wdk_ref, wdv_ref, wuq_ref, wukv_ref,
                 gq_ref, gkv_ref, q_ref, k_ref, v_ref, dq_ref, dk_ref, dv_ref, kpos_ref):
    xb = x_ref[...].astype(jnp.bfloat16)
    scale = (MLA_NOPE_DIM + MLA_ROPE_DIM) ** -0.5 * LOG2E
    cos_s, sin_s, cos, sin, kpos = _token_tables(pos_ref[0].astype(jnp.float32), invf_ref[...], scale)
    kpos_ref[...] = kpos.astype(kpos_ref.dtype)

    lat = _dot_nt(xb, wlat_ref[...])

    cq = _rms(lat[:, :_O1], gq_ref[...]).astype(jnp.bfloat16)
    qq = _dot(cq, wuq_ref[...])
    half = MLA_HEADS * HEAD_SLAB
    per_group = LANES // MLA_ROPE_DIM
    for h in range(MLA_HEADS):
        lo = h * HEAD_SLAB
        grp = half + (h // per_group) * LANES
        rot = pltpu.roll(qq[:, grp:grp + LANES], (MLA_NOPE_DIM - (h % per_group) * MLA_ROPE_DIM) % LANES, 1)
        q_ref[:, lo:lo + HEAD_SLAB] = (qq[:, lo:lo + HEAD_SLAB] * cos_s + rot * sin_s).astype(q_ref.dtype)

    kp = _dot_nt(xb, wkpe_ref[...])
    kpe = kp[:, :HEAD_SLAB] * cos + kp[:, HEAD_SLAB:] * sin

    ckv = _rms(lat[:, _O1:_O2], gkv_ref[...]).astype(jnp.bfloat16)
    kv = _dot(ckv, wukv_ref[...])
    for h in range(MLA_HEADS):
        lo = h * HEAD_SLAB
        k_ref[:, lo:lo + HEAD_SLAB] = (kv[:, lo:lo + HEAD_SLAB] + kpe).astype(k_ref.dtype)
    v_ref[...] = kv[:, half:].astype(v_ref.dtype)

    dq_ref[...] = (_dot_nt(xb, wdq_ref[...]) * (DIFF_HEAD_DIM ** -0.5 * LOG2E)).astype(dq_ref.dtype)
    dk_ref[...] = _dot_nt(xb, wdk_ref[...]).astype(dk_ref.dtype)
    dv_ref[...] = _dot_nt(xb, wdv_ref[...]).astype(dv_ref.dtype)


def _proj(x2, pos_rows, invf, w_in_t, wkpe_t, wuq, wukv, gq, gkv):
    t, d = x2.shape
    rows = PROJ_ROWS
    bf = jnp.bfloat16
    tok = lambda w: pl.BlockSpec((rows, w), lambda i: (i, 0))
    wrows = lambda a, b: _resident_rows(a, b, d)
    out_shape = [jax.ShapeDtypeStruct((t, MLA_HEADS * HEAD_SLAB), bf),
                 jax.ShapeDtypeStruct((t, MLA_HEADS * HEAD_SLAB), bf),
                 jax.ShapeDtypeStruct((t, MLA_HEADS * MLA_V_DIM), bf),
                 jax.ShapeDtypeStruct((t, D_MODEL), bf),
                 jax.ShapeDtypeStruct((t, D_MODEL), bf),
                 jax.ShapeDtypeStruct((t, D_MODEL), bf),
                 jax.ShapeDtypeStruct((t, LANES), bf)]
    return pl.pallas_call(
        _proj_kernel,
        grid=(t // rows,),
        in_specs=[tok(D_MODEL), pl.BlockSpec((1, 1, rows), lambda i: (i, 0, 0)), _resident(invf.shape),
                  wrows(0, _O2), _resident(wkpe_t.shape), wrows(_O3, _O4), wrows(_O4, _O5), wrows(_O5, _O6),
                  _resident(wuq.shape), _resident(wukv.shape), _resident(gq.shape), _resident(gkv.shape)],
        out_specs=[tok(s.shape[1]) for s in out_shape],
        out_shape=out_shape,
        compiler_params=pltpu.CompilerParams(dimension_semantics=("parallel",), vmem_limit_bytes=VMEM_LIMIT),
        name="proj",
    )(x2, pos_rows, invf, w_in_t, wkpe_t, w_in_t, w_in_t, w_in_t, wuq, wukv, gq, gkv)


def _tile_order(n):
    return list(range(n - 1, -1, -1))


def _causal_mask(tq, tk):
    row = lax.broadcasted_iota(jnp.int32, (tq, tk), 0)
    col = lax.broadcasted_iota(jnp.int32, (tq, tk), 1)
    return col <= row


def _tile_attention(scores, vaug_ref, r0, mask):
    t = mask.shape[1]
    pv = lambda s, m, lo, hi: _dot(jnp.exp2(s - m).astype(jnp.bfloat16), vaug_ref[lo:hi, :])
    s_d = jnp.where(mask, scores(r0, r0 + t), NEG_INF)
    m = jnp.max(s_d, axis=-1, keepdims=True)
    if r0 > 0:
        s_p = scores(0, r0)
        m = jnp.maximum(m, jnp.max(s_p, axis=-1, keepdims=True))
    acc = pv(s_d, m, r0, r0 + t)
    if r0 > 0:
        acc = acc + pv(s_p, m, 0, r0)
    return acc[:, :LANES] / acc[:, LANES:]


def _mla_kernel(q_ref, k_ref, v_ref, o_ref, vaug_ref):
    tq = MLA_TILE
    seq = q_ref.shape[1]
    for p in range(MLA_PAIRS):
        vaug_ref[p, :, :LANES] = v_ref[0, :, p * LANES:(p + 1) * LANES]
        vaug_ref[p, :, LANES:] = jnp.ones((seq, LANES), vaug_ref.dtype)
    mask = _causal_mask(tq, tq)
    lane = lax.broadcasted_iota(jnp.int32, (tq, 2 * MLA_V_DIM), 1)
    for i in _tile_order(seq // tq):
        r0 = i * tq
        for p in range(MLA_PAIRS):
            outs = []
            for hh in range(2):
                lo = (2 * p + hh) * HEAD_SLAB
                q = q_ref[0, r0:r0 + tq, lo:lo + HEAD_SLAB]
                scores = lambda a, b, q=q, lo=lo: _dot_nt(q, k_ref[0, a:b, lo:lo + HEAD_SLAB])
                outs.append(_tile_attention(scores, vaug_ref.at[p], r0, mask))
            o_ref[0, r0:r0 + tq, p * LANES:(p + 1) * LANES] = jnp.where(lane < MLA_V_DIM, outs[0],
                                                                        outs[1]).astype(o_ref.dtype)


def _mla_attention(q, k, v):
    b, s, _ = q.shape
    steps = MLA_HEADS // (2 * MLA_PAIRS)
    blk = lambda w: pl.BlockSpec((1, s, MLA_PAIRS * w), lambda bi, p: (bi, 0, p))
    return pl.pallas_call(
        _mla_kernel,
        grid=(b, steps),
        in_specs=[blk(2 * HEAD_SLAB), blk(2 * HEAD_SLAB), blk(2 * MLA_V_DIM)],
        out_specs=blk(2 * MLA_V_DIM),
        out_shape=jax.ShapeDtypeStruct((b, s, MLA_HEADS * MLA_V_DIM), jnp.bfloat16),
        scratch_shapes=[pltpu.VMEM((MLA_PAIRS, s, 2 * LANES), jnp.bfloat16)],
        compiler_params=pltpu.CompilerParams(dimension_semantics=("parallel", "parallel"),
                                             vmem_limit_bytes=VMEM_LIMIT),
        name="mla_attn",
    )(q, k, v)


def _diff_kernel(slopes_ref, ok_ref, q_ref, k_ref, v_ref, kpos_ref, pk_ref, lq1_ref, lk1_ref, lq2_ref,
                 lk2_ref, g_ref, o_ref, vaug_ref, kaug_ref):
    tq = DIFF_TILE
    seq = q_ref.shape[1]
    head_lanes = lambda j: slice(j * LANES, (j + 1) * LANES)
    for j in range(DIFF_GROUP):
        vaug_ref[j, :, :LANES] = v_ref[0, :, head_lanes(j)]
        vaug_ref[j, :, LANES:] = jnp.ones((seq, LANES), vaug_ref.dtype)
    both = lambda a: jnp.concatenate([a, a], axis=0)
    mask = both(_causal_mask(tq, tq))
    slopes = [slopes_ref[pl.program_id(1) * DIFF_GROUP + j] * LOG2E for j in range(DIFF_GROUP)]
    lane = lax.broadcasted_iota(jnp.int32, (tq, LANES), 1)
    lam = (jnp.exp(jnp.sum(lq1_ref[...] * lk1_ref[...], axis=-1, keepdims=True))
           - jnp.exp(jnp.sum(lq2_ref[...] * lk2_ref[...], axis=-1, keepdims=True)) + LAMBDA_INIT)

    def tiles(fast):
        coefs = []
        if fast:
            for j in range(DIFF_GROUP):
                kaug_ref[j, :, :LANES] = k_ref[0, :, head_lanes(j)]
                kaug_ref[j, :, LANES:] = kpos_ref[0]
                coef = _piece_pattern(_bf16_pieces(jnp.full((1, 1), slopes[j], jnp.float32)), True, 1, (1, LANES))
                coefs.append(jnp.broadcast_to(coef.astype(jnp.bfloat16), (2 * tq, LANES)))
        for i in _tile_order(seq // tq):
            r0 = i * tq
            for j in range(DIFF_GROUP):
                q = q_ref[0, r0:r0 + tq, head_lanes(j)]
                zero = jnp.zeros_like(q)
                qq = jnp.concatenate([jnp.where(lane < DIFF_HEAD_DIM, q, zero),
                                      jnp.where(lane < DIFF_HEAD_DIM, zero, q)], axis=0)
                if fast:
                    qq = jnp.concatenate([qq, coefs[j]], axis=1)
                    scores = lambda a, b, qq=qq, j=j: _dot_nt(qq, kaug_ref[j, a:b, :])
                else:
                    pq = jnp.broadcast_to(pk_ref[0, :, r0:r0 + tq], (LANES, tq)).T[:, :1]

                    def scores(a, b, qq=qq, pq=pq, j=j):
                        dist = jnp.abs(pq - pk_ref[0, :, a:b]).astype(jnp.float32)
                        return _dot_nt(qq, k_ref[0, a:b, head_lanes(j)]) - both(dist * slopes[j])

                out = _tile_attention(scores, vaug_ref.at[j], r0, mask)
                o = out[:tq] - lam * out[tq:]
                o_ref[0, r0:r0 + tq, head_lanes(j)] = (_rms(o, g_ref[...])
                                                       * (1.0 - LAMBDA_INIT)).astype(o_ref.dtype)

    ok = ok_ref[pl.program_id(0)]
    pl.when(ok == 1)(functools.partial(tiles, True))
    pl.when(ok != 1)(functools.partial(tiles, False))


def _diff_attention(slopes, ok, q, k, v, kpos, pos_row, lq1, lk1, lq2, lk2, g):
    b, s, _ = q.shape
    vec = lambda a: pl.BlockSpec(a.shape, lambda bi, h: (0, 0))
    head = pl.BlockSpec((1, s, DIFF_GROUP * LANES), lambda bi, h: (bi, 0, h))
    smem = pl.BlockSpec(memory_space=pltpu.SMEM)
    return pl.pallas_call(
        _diff_kernel,
        grid=(b, DIFF_HEADS // DIFF_GROUP),
        in_specs=[smem, smem, head, head, head,
                  pl.BlockSpec((1, s, LANES), lambda bi, h: (bi, 0, 0)),
                  pl.BlockSpec((1, 1, s), lambda bi, h: (bi, 0, 0)),
                  vec(lq1), vec(lk1), vec(lq2), vec(lk2), vec(g)],
        out_specs=head,
        out_shape=jax.ShapeDtypeStruct((b, s, DIFF_HEADS * DIFF_V_DIM), jnp.bfloat16),
        scratch_shapes=[pltpu.VMEM((DIFF_GROUP, s, 2 * LANES), jnp.bfloat16)] * 2,
        compiler_params=pltpu.CompilerParams(dimension_semantics=("parallel", "parallel"),
                                             vmem_limit_bytes=VMEM_LIMIT),
        name="diff_attn",
    )(slopes, ok, q, k, v, kpos, pos_row, lq1, lk1, lq2, lk2, g)


def _post_kernel(x_ref, om_ref, od_ref, wg_ref, bg_ref, wom_ref, wod_ref, wo_ref, g1_ref, b1_ref,
                 wup_ref, wdn_ref, g2_ref, b2_ref, o_ref):
    def mix(rows):
        x = x_ref[rows, :]
        xb = x.astype(jnp.bfloat16)
        gate = jax.nn.sigmoid(_dot_nt(xb, wg_ref[...]) + bg_ref[...])
        y_mla = _dot(om_ref[rows, :], wom_ref[...])
        y_diff = _dot(od_ref[rows, :], wod_ref[...])
        merged = gate[:, :D_MODEL] * y_mla + gate[:, D_MODEL:] * y_diff
        mixed = _dot(merged.astype(jnp.bfloat16), wo_ref[...])
        return _layer_norm(DEEPNORM_ALPHA * x + mixed, g1_ref[...], b1_ref[...])

    def mlp(rows, x1):
        x1b = x1.astype(jnp.bfloat16)
        acc = DEEPNORM_ALPHA * x1
        for c in range(D_FF // FF_CHUNK):
            lo = c * FF_CHUNK
            hid = jnp.maximum(_dot(x1b, wup_ref[:, lo:lo + FF_CHUNK]), 0.0)
            acc = acc + _dot((hid * hid).astype(jnp.bfloat16), wdn_ref[lo:lo + FF_CHUNK, :])
        o_ref[rows, :] = _layer_norm(acc, g2_ref[...], b2_ref[...]).astype(o_ref.dtype)

    sub = x_ref.shape[0] // POST_GROUPS
    groups = [slice(g * sub, (g + 1) * sub) for g in range(POST_GROUPS)]
    x1s = [mix(rows) for rows in groups]
    for rows, x1 in zip(groups, x1s):
        mlp(rows, x1)


def _post(x2, o_mla, o_diff, w_in_t, bg, wom, wod, wo, g1, b1, wup, wdn, g2, b2):
    t, d = x2.shape
    rows = POST_ROWS
    tok = lambda w: pl.BlockSpec((rows, w), lambda i: (i, 0))
    consts = (bg, wom, wod, wo, g1, b1, wup, wdn, g2, b2)
    return pl.pallas_call(
        _post_kernel,
        grid=(t // rows,),
        in_specs=([tok(D_MODEL), tok(o_mla.shape[1]), tok(o_diff.shape[1]), _resident_rows(_O6, w_in_t.shape[0], d)]
                  + [_resident(a.shape) for a in consts]),
        out_specs=tok(D_MODEL),
        out_shape=jax.ShapeDtypeStruct((t, D_MODEL), jnp.float32),
        compiler_params=pltpu.CompilerParams(dimension_semantics=("parallel",), vmem_limit_bytes=VMEM_LIMIT),
        name="post",
    )(x2, o_mla, o_diff, w_in_t, *consts)


def _rot_half_cols(w):
    half = w.shape[-1] // 2
    return jnp.concatenate([-w[..., half:], w[..., :half]], axis=-1)


def _prep_weights(w_in, w_uq, w_ukv):
    bf = jnp.bfloat16
    f32 = jnp.float32
    pad_r = HEAD_SLAB - MLA_NOPE_DIM - MLA_ROPE_DIM
    d = w_in.shape[0]
    w_in_t = jnp.swapaxes(w_in, 0, 1).astype(bf)
    wkpe = w_in[:, _O2:_O3]
    slab = lambda w: jnp.concatenate([jnp.zeros((d, MLA_NOPE_DIM), f32), w, jnp.zeros((d, pad_r), f32)], axis=1)
    wkpe_t = jnp.concatenate([slab(wkpe), slab(_rot_half_cols(wkpe))], axis=1).T.astype(bf)

    r = w_uq.shape[0]
    uq = w_uq.reshape(r, MLA_HEADS, MLA_NOPE_DIM + MLA_ROPE_DIM)
    nope, pe = uq[..., :MLA_NOPE_DIM], uq[..., MLA_NOPE_DIM:]
    plain = jnp.concatenate([nope, pe, jnp.zeros((r, MLA_HEADS, pad_r), f32)], axis=-1)
    wuq = jnp.concatenate([plain.reshape(r, -1), _rot_half_cols(pe).reshape(r, -1)], axis=1).astype(bf)

    r = w_ukv.shape[0]
    ukv = w_ukv.reshape(r, MLA_HEADS, MLA_NOPE_DIM + MLA_V_DIM)
    kn, vv = ukv[..., :MLA_NOPE_DIM], ukv[..., MLA_NOPE_DIM:]
    kslab = jnp.concatenate([kn, jnp.zeros((r, MLA_HEADS, HEAD_SLAB - MLA_NOPE_DIM), f32)], axis=-1)
    wukv = jnp.concatenate([kslab.reshape(r, -1), vv.reshape(r, -1)], axis=1).astype(bf)
    return w_in_t, wkpe_t, wuq, wukv


def kernel(x, positions, w_in, b_gate, mla_q_norm, mla_kv_norm, w_uq, w_ukv, w_o_mla, diff_lambda_q1, diff_lambda_k1, diff_lambda_q2, diff_lambda_k2, diff_subln, w_o_diff, w_o, ln1_g, ln1_b, w_up, w_down, ln2_g, ln2_b):
    b, s, d = x.shape
    bf = jnp.bfloat16
    layer = 0
    w_in_t, wkpe_t, wuq, wukv = _prep_weights(w_in[layer], w_uq[layer], w_ukv[layer])

    inv_freq = 1.0 / (ROPE_THETA ** (jnp.arange(0, MLA_ROPE_DIM, 2, dtype=jnp.float32) / MLA_ROPE_DIM))
    x2 = x.reshape(b * s, d)
    q, k, v, dq, dk, dv, kpos = _proj(x2, positions.reshape(b * s // PROJ_ROWS, 1, PROJ_ROWS), inv_freq[:, None],
                                      w_in_t, wkpe_t, wuq, wukv, mla_q_norm[layer][None, :],
                                      mla_kv_norm[layer][None, :])
    r3 = lambda a: a.reshape(b, s, a.shape[-1])
    o_mla = _mla_attention(r3(q), r3(k), r3(v))

    slopes = jnp.asarray(_alibi_slopes(DIFF_HEADS))
    ordered = jnp.all(positions[:, 1:] >= positions[:, :-1], axis=1)
    exact_f32 = jnp.all((positions < F32_EXACT_INT) & (positions > -F32_EXACT_INT), axis=1)
    ok = (ordered & exact_f32).astype(jnp.int32)
    o_diff = _diff_attention(slopes, ok, r3(dq), r3(dk), r3(dv), r3(kpos), positions.reshape(b, 1, s),
                             diff_lambda_q1[layer][None, :], diff_lambda_k1[layer][None, :],
                             diff_lambda_q2[layer][None, :], diff_lambda_k2[layer][None, :],
                             diff_subln[layer][None, :])

    row = lambda a: a[layer][None, :]
    out = _post(x2, o_mla.reshape(b * s, -1), o_diff.reshape(b * s, -1), w_in_t, row(b_gate),
                w_o_mla[layer].astype(bf), w_o_diff[layer].astype(bf), w_o[layer].astype(bf),
                row(ln1_g), row(ln1_b), w_up[layer].astype(bf), w_down[layer].astype(bf), row(ln2_g), row(ln2_b))
    return out.reshape(b, s, d)
```

```python
import functools
import math

import jax
import jax.numpy as jnp
import numpy as np
from jax import lax
from jax.experimental import pallas as pl
from jax.experimental.pallas import tpu as pltpu

D_MODEL = 1024
MLA_HEADS = 8
MLA_NOPE_DIM = 64
MLA_ROPE_DIM = 32
MLA_V_DIM = 64
MLA_Q_RANK = 384
MLA_KV_RANK = 256
ROPE_THETA = 10000.0
DIFF_HEADS = 8
DIFF_HEAD_DIM = 64
DIFF_V_DIM = 128
D_FF = 4 * D_MODEL
DEPTH = 1
LN_EPS = 1e-5
RMS_EPS = 1e-6
NEG_INF = -1e30
DEEPNORM_ALPHA = (2.0 * DEPTH) ** 0.25
LAMBDA_INIT = 0.8 - 0.6 * math.exp(-0.3 * 0)
LOG2E = math.log2(math.e)

LANES = 128
HEAD_SLAB = LANES
VMEM_LIMIT = 56 * 1024 * 1024
PIECES = 3
ORDERED_SPAN_LIMIT = 1 << 14

PROJ_ROWS = 512
MLA_TILE = 512
MLA_PAIRS = 4
DIFF_GROUP = 1
DIFF_TILE = 256
POST_ROWS = 512
POST_GROUPS = 2
FF_CHUNK = 1024

_O1 = MLA_Q_RANK
_O2 = _O1 + MLA_KV_RANK
_O3 = _O2 + MLA_ROPE_DIM
_O4 = _O3 + DIFF_HEADS * 2 * DIFF_HEAD_DIM
_O5 = _O4 + DIFF_HEADS * 2 * DIFF_HEAD_DIM
_O6 = _O5 + DIFF_HEADS * DIFF_V_DIM


def _alibi_slopes(n):
    def pow2_slopes(k):
        start = 2.0 ** (-8.0 / k)
        return [start ** (i + 1) for i in range(k)]
    if math.log2(n).is_integer():
        s = pow2_slopes(n)
    else:
        c = 2 ** int(math.floor(math.log2(n)))
        s = pow2_slopes(c) + pow2_slopes(2 * c)[0::2][: n - c]
    return np.asarray(s, dtype=np.float32)


def _resident(shape):
    return pl.BlockSpec(shape, lambda *_: (0,) * len(shape), pipeline_mode=pl.Buffered(1))


def _resident_rows(start, stop, width):
    return pl.BlockSpec((pl.Element(stop - start), pl.Element(width)), lambda *_: (start, 0),
                        pipeline_mode=pl.Buffered(1))


def _rms(x, g):
    ms = jnp.mean(x * x, axis=-1, keepdims=True)
    return x * lax.rsqrt(ms + RMS_EPS) * g


def _layer_norm(x, g, b):
    mu = jnp.mean(x, axis=-1, keepdims=True)
    xc = x - mu
    var = jnp.mean(xc * xc, axis=-1, keepdims=True)
    return xc * lax.rsqrt(var + LN_EPS) * g + b


def _dot(a, b):
    return jnp.dot(a, b, preferred_element_type=jnp.float32)


def _dot_nt(a, b):
    return lax.dot_general(a, b, (((1,), (1,)), ((), ())), preferred_element_type=jnp.float32)


def _bf16_pieces(x):
    out = []
    for _ in range(PIECES):
        p = x.astype(jnp.bfloat16).astype(jnp.float32)
        out.append(p)
        x = x - p
    return out


def _piece_pattern(pieces, by_group, axis, shape):
    idx = lax.broadcasted_iota(jnp.int32, shape, axis)
    out = jnp.zeros(shape, jnp.float32)
    for a in range(PIECES):
        for b in range(PIECES):
            out = jnp.where(idx == PIECES * a + b, pieces[a if by_group else b], out)
    return out


def _token_tables(pos_row, rel_row, invf_col, scale):
    rows = pos_row.shape[1]
    half_r = MLA_ROPE_DIM // 2
    ang = invf_col * pos_row
    cos, sin = jnp.cos(ang), jnp.sin(ang)
    pad = jnp.zeros((HEAD_SLAB - MLA_NOPE_DIM - MLA_ROPE_DIM, rows), jnp.float32)
    nope0 = jnp.zeros((MLA_NOPE_DIM, rows), jnp.float32)
    slab = lambda nope, t: jnp.concatenate([nope, t, t, pad], axis=0).T
    pieces = _piece_pattern(_bf16_pieces(rel_row), False, 0, (half_r, rows))
    kpos = jnp.concatenate([pieces, jnp.zeros((LANES - half_r, rows), jnp.float32)], axis=0).T
    return (slab(nope0 + scale, cos * scale), slab(nope0, sin * scale), slab(nope0, cos), slab(nope0, sin), kpos)


def _proj_kernel(x_ref, pos_ref, invf_ref, wlat_ref, wkpe_ref, wdq_ref, wdk_ref, wdv_ref, wuq_ref, wukv_ref,
                 gq_ref, gkv_ref, q_ref, k_ref, v_ref, dq_ref, dk_ref, dv_ref, kpos_ref):
    xb = x_ref[...].astype(jnp.bfloat16)
    scale = (MLA_NOPE_DIM + MLA_ROPE_DIM) ** -0.5 * LOG2E
    pos = pos_ref[0].astype(jnp.float32)
    cos_s, sin_s, cos, sin, kpos = _token_tables(pos[0:1], pos[1:2], invf_ref[...], scale)
    kpos_ref[...] = kpos.astype(kpos_ref.dtype)

    lat = _dot_nt(xb, wlat_ref[...])

    cq = _rms(lat[:, :_O1], gq_ref[...]).astype(jnp.bfloat16)
    qq = _dot(cq, wuq_ref[...])
    half = MLA_HEADS * HEAD_SLAB
    per_group = LANES // MLA_ROPE_DIM
    for h in range(MLA_HEADS):
        lo = h * HEAD_SLAB
        grp = half + (h // per_group) * LANES
        rot = pltpu.roll(qq[:, grp:grp + LANES], (MLA_NOPE_DIM - (h % per_group) * MLA_ROPE_DIM) % LANES, 1)
        q_ref[:, lo:lo + HEAD_SLAB] = (qq[:, lo:lo + HEAD_SLAB] * cos_s + rot * sin_s).astype(q_ref.dtype)

    kp = _dot_nt(xb, wkpe_ref[...])
    kpe = kp[:, :HEAD_SLAB] * cos + kp[:, HEAD_SLAB:] * sin

    ckv = _rms(lat[:, _O1:_O2], gkv_ref[...]).astype(jnp.bfloat16)
    kv = _dot(ckv, wukv_ref[...])
    for h in range(MLA_HEADS):
        lo = h * HEAD_SLAB
        k_ref[:, lo:lo + HEAD_SLAB] = (kv[:, lo:lo + HEAD_SLAB] + kpe).astype(k_ref.dtype)
    v_ref[...] = kv[:, half:].astype(v_ref.dtype)

    dq_ref[...] = (_dot_nt(xb, wdq_ref[...]) * (DIFF_HEAD_DIM ** -0.5 * LOG2E)).astype(dq_ref.dtype)
    dk_ref[...] = _dot_nt(xb, wdk_ref[...]).astype(dk_ref.dtype)
    dv_ref[...] = _dot_nt(xb, wdv_ref[...]).astype(dv_ref.dtype)


def _proj(x2, pos_rows, invf, w_in_t, wkpe_t, wuq, wukv, gq, gkv):
    t, d = x2.shape
    rows = PROJ_ROWS
    bf = jnp.bfloat16
    tok = lambda w: pl.BlockSpec((rows, w), lambda i: (i, 0))
    wrows = lambda a, b: _resident_rows(a, b, d)
    out_shape = [jax.ShapeDtypeStruct((t, MLA_HEADS * HEAD_SLAB), bf),
                 jax.ShapeDtypeStruct((t, MLA_HEADS * HEAD_SLAB), bf),
                 jax.ShapeDtypeStruct((t, MLA_HEADS * MLA_V_DIM), bf),
                 jax.ShapeDtypeStruct((t, D_MODEL), bf),
                 jax.ShapeDtypeStruct((t, D_MODEL), bf),
                 jax.ShapeDtypeStruct((t, D_MODEL), bf),
                 jax.ShapeDtypeStruct((t, LANES), bf)]
    return pl.pallas_call(
        _proj_kernel,
        grid=(t // rows,),
        in_specs=[tok(D_MODEL), pl.BlockSpec((1, 2, rows), lambda i: (i, 0, 0)), _resident(invf.shape),
                  wrows(0, _O2), _resident(wkpe_t.shape), wrows(_O3, _O4), wrows(_O4, _O5), wrows(_O5, _O6),
                  _resident(wuq.shape), _resident(wukv.shape), _resident(gq.shape), _resident(gkv.shape)],
        out_specs=[tok(s.shape[1]) for s in out_shape],
        out_shape=out_shape,
        compiler_params=pltpu.CompilerParams(dimension_semantics=("parallel",), vmem_limit_bytes=VMEM_LIMIT),
        name="proj",
    )(x2, pos_rows, invf, w_in_t, wkpe_t, w_in_t, w_in_t, w_in_t, wuq, wukv, gq, gkv)


def _tile_order(n):
    return list(range(n - 1, -1, -1))


def _causal_mask(tq, tk):
    row = lax.broadcasted_iota(jnp.int32, (tq, tk), 0)
    col = lax.broadcasted_iota(jnp.int32, (tq, tk), 1)
    return col <= row


def _tile_attention(scores, vaug_ref, r0, mask):
    t = mask.shape[1]
    pv = lambda s, m, lo, hi: _dot(jnp.exp2(s - m).astype(jnp.bfloat16), vaug_ref[lo:hi, :])
    s_d = jnp.where(mask, scores(r0, r0 + t), NEG_INF)
    m = jnp.max(s_d, axis=-1, keepdims=True)
    if r0 > 0:
        s_p = scores(0, r0)
        m = jnp.maximum(m, jnp.max(s_p, axis=-1, keepdims=True))
    acc = pv(s_d, m, r0, r0 + t)
    if r0 > 0:
        acc = acc + pv(s_p, m, 0, r0)
    return acc[:, :LANES] / acc[:, LANES:]


def _mla_kernel(q_ref, k_ref, v_ref, o_ref, vaug_ref):
    tq = MLA_TILE
    seq = q_ref.shape[1]
    for p in range(MLA_PAIRS):
        vaug_ref[p, :, :LANES] = v_ref[0, :, p * LANES:(p + 1) * LANES]
        vaug_ref[p, :, LANES:] = jnp.ones((seq, LANES), vaug_ref.dtype)
    mask = _causal_mask(tq, tq)
    lane = lax.broadcasted_iota(jnp.int32, (tq, 2 * MLA_V_DIM), 1)
    for i in _tile_order(seq // tq):
        r0 = i * tq
        for p in range(MLA_PAIRS):
            outs = []
            for hh in range(2):
                lo = (2 * p + hh) * HEAD_SLAB
                q = q_ref[0, r0:r0 + tq, lo:lo + HEAD_SLAB]
                scores = lambda a, b, q=q, lo=lo: _dot_nt(q, k_ref[0, a:b, lo:lo + HEAD_SLAB])
                outs.append(_tile_attention(scores, vaug_ref.at[p], r0, mask))
            o_ref[0, r0:r0 + tq, p * LANES:(p + 1) * LANES] = jnp.where(lane < MLA_V_DIM, outs[0],
                                                                        outs[1]).astype(o_ref.dtype)


def _mla_attention(q, k, v):
    b, s, _ = q.shape
    steps = MLA_HEADS // (2 * MLA_PAIRS)
    blk = lambda w: pl.BlockSpec((1, s, MLA_PAIRS * w), lambda bi, p: (bi, 0, p))
    return pl.pallas_call(
        _mla_kernel,
        grid=(b, steps),
        in_specs=[blk(2 * HEAD_SLAB), blk(2 * HEAD_SLAB), blk(2 * MLA_V_DIM)],
        out_specs=blk(2 * MLA_V_DIM),
        out_shape=jax.ShapeDtypeStruct((b, s, MLA_HEADS * MLA_V_DIM), jnp.bfloat16),
        scratch_shapes=[pltpu.VMEM((MLA_PAIRS, s, 2 * LANES), jnp.bfloat16)],
        compiler_params=pltpu.CompilerParams(dimension_semantics=("parallel", "parallel"),
                                             vmem_limit_bytes=VMEM_LIMIT),
        name="mla_attn",
    )(q, k, v)


def _diff_kernel(slopes_ref, ok_ref, q_ref, k_ref, v_ref, kpos_ref, pk_ref, lq1_ref, lk1_ref, lq2_ref,
                 lk2_ref, g_ref, o_ref, vaug_ref, kaug_ref):
    tq = DIFF_TILE
    seq = q_ref.shape[1]
    head_lanes = lambda j: slice(j * LANES, (j + 1) * LANES)
    for j in range(DIFF_GROUP):
        vaug_ref[j, :, :LANES] = v_ref[0, :, head_lanes(j)]
        vaug_ref[j, :, LANES:] = jnp.ones((seq, LANES), vaug_ref.dtype)
    both = lambda a: jnp.concatenate([a, a], axis=0)
    mask = both(_causal_mask(tq, tq))
    slopes = [slopes_ref[pl.program_id(1) * DIFF_GROUP + j] * LOG2E for j in range(DIFF_GROUP)]
    lane = lax.broadcasted_iota(jnp.int32, (tq, LANES), 1)
    lam = (jnp.exp(jnp.sum(lq1_ref[...] * lk1_ref[...], axis=-1, keepdims=True))
           - jnp.exp(jnp.sum(lq2_ref[...] * lk2_ref[...], axis=-1, keepdims=True)) + LAMBDA_INIT)

    def tiles(fast):
        coefs = []
        if fast:
            for j in range(DIFF_GROUP):
                kaug_ref[j, :, :LANES] = k_ref[0, :, head_lanes(j)]
                kaug_ref[j, :, LANES:] = kpos_ref[0]
                coef = _piece_pattern(_bf16_pieces(jnp.full((1, 1), slopes[j], jnp.float32)), True, 1, (1, LANES))
                coefs.append(jnp.broadcast_to(coef.astype(jnp.bfloat16), (2 * tq, LANES)))
        for i in _tile_order(seq // tq):
            r0 = i * tq
            for j in range(DIFF_GROUP):
                q = q_ref[0, r0:r0 + tq, head_lanes(j)]
                zero = jnp.zeros_like(q)
                qq = jnp.concatenate([jnp.where(lane < DIFF_HEAD_DIM, q, zero),
                                      jnp.where(lane < DIFF_HEAD_DIM, zero, q)], axis=0)
                if fast:
                    qq = jnp.concatenate([qq, coefs[j]], axis=1)
                    scores = lambda a, b, qq=qq, j=j: _dot_nt(qq, kaug_ref[j, a:b, :])
                else:
                    pq = jnp.broadcast_to(pk_ref[0, :, r0:r0 + tq], (LANES, tq)).T[:, :1]

                    def scores(a, b, qq=qq, pq=pq, j=j):
                        dist = jnp.abs(pq - pk_ref[0, :, a:b]).astype(jnp.float32)
                        return _dot_nt(qq, k_ref[0, a:b, head_lanes(j)]) - both(dist * slopes[j])

                out = _tile_attention(scores, vaug_ref.at[j], r0, mask)
                o = out[:tq] - lam * out[tq:]
                o_ref[0, r0:r0 + tq, head_lanes(j)] = (_rms(o, g_ref[...])
                                                       * (1.0 - LAMBDA_INIT)).astype(o_ref.dtype)

    ok = ok_ref[pl.program_id(0)]
    pl.when(ok == 1)(functools.partial(tiles, True))
    pl.when(ok != 1)(functools.partial(tiles, False))


def _diff_attention(slopes, ok, q, k, v, kpos, pos_row, lq1, lk1, lq2, lk2, g):
    b, s, _ = q.shape
    vec = lambda a: pl.BlockSpec(a.shape, lambda bi, h: (0, 0))
    head = pl.BlockSpec((1, s, DIFF_GROUP * LANES), lambda bi, h: (bi, 0, h))
    smem = pl.BlockSpec(memory_space=pltpu.SMEM)
    return pl.pallas_call(
        _diff_kernel,
        grid=(b, DIFF_HEADS // DIFF_GROUP),
        in_specs=[smem, smem, head, head, head,
                  pl.BlockSpec((1, s, LANES), lambda bi, h: (bi, 0, 0)),
                  pl.BlockSpec((1, 1, s), lambda bi, h: (bi, 0, 0)),
                  vec(lq1), vec(lk1), vec(lq2), vec(lk2), vec(g)],
        out_specs=head,
        out_shape=jax.ShapeDtypeStruct((b, s, DIFF_HEADS * DIFF_V_DIM), jnp.bfloat16),
        scratch_shapes=[pltpu.VMEM((DIFF_GROUP, s, 2 * LANES), jnp.bfloat16)] * 2,
        compiler_params=pltpu.CompilerParams(dimension_semantics=("parallel", "parallel"),
                                             vmem_limit_bytes=VMEM_LIMIT),
        name="diff_attn",
    )(slopes, ok, q, k, v, kpos, pos_row, lq1, lk1, lq2, lk2, g)


def _post_kernel(x_ref, om_ref, od_ref, wg_ref, bg_ref, wom_ref, wod_ref, wo_ref, g1_ref, b1_ref,
                 wup_ref, wdn_ref, g2_ref, b2_ref, o_ref):
    def mix(rows):
        x = x_ref[rows, :]
        xb = x.astype(jnp.bfloat16)
        gate = jax.nn.sigmoid(_dot_nt(xb, wg_ref[...]) + bg_ref[...])
        y_mla = _dot(om_ref[rows, :], wom_ref[...])
        y_diff = _dot(od_ref[rows, :], wod_ref[...])
        merged = gate[:, :D_MODEL] * y_mla + gate[:, D_MODEL:] * y_diff
        mixed = _dot(merged.astype(jnp.bfloat16), wo_ref[...])
        return _layer_norm(DEEPNORM_ALPHA * x + mixed, g1_ref[...], b1_ref[...])

    def mlp(rows, x1):
        x1b = x1.astype(jnp.bfloat16)
        acc = DEEPNORM_ALPHA * x1
        for c in range(D_FF // FF_CHUNK):
            lo = c * FF_CHUNK
            hid = jnp.maximum(_dot(x1b, wup_ref[:, lo:lo + FF_CHUNK]), 0.0)
            acc = acc + _dot((hid * hid).astype(jnp.bfloat16), wdn_ref[lo:lo + FF_CHUNK, :])
        o_ref[rows, :] = _layer_norm(acc, g2_ref[...], b2_ref[...]).astype(o_ref.dtype)

    sub = x_ref.shape[0] // POST_GROUPS
    groups = [slice(g * sub, (g + 1) * sub) for g in range(POST_GROUPS)]
    x1s = [mix(rows) for rows in groups]
    for rows, x1 in zip(groups, x1s):
        mlp(rows, x1)


def _post(x2, o_mla, o_diff, w_in_t, bg, wom, wod, wo, g1, b1, wup, wdn, g2, b2):
    t, d = x2.shape
    rows = POST_ROWS
    tok = lambda w: pl.BlockSpec((rows, w), lambda i: (i, 0))
    consts = (bg, wom, wod, wo, g1, b1, wup, wdn, g2, b2)
    return pl.pallas_call(
        _post_kernel,
        grid=(t // rows,),
        in_specs=([tok(D_MODEL), tok(o_mla.shape[1]), tok(o_diff.shape[1]), _resident_rows(_O6, w_in_t.shape[0], d)]
                  + [_resident(a.shape) for a in consts]),
        out_specs=tok(D_MODEL),
        out_shape=jax.ShapeDtypeStruct((t, D_MODEL), jnp.float32),
        compiler_params=pltpu.CompilerParams(dimension_semantics=("parallel",), vmem_limit_bytes=VMEM_LIMIT),
        name="post",
    )(x2, o_mla, o_diff, w_in_t, *consts)


def _rot_half_cols(w):
    half = w.shape[-1] // 2
    return jnp.concatenate([-w[..., half:], w[..., :half]], axis=-1)


def _prep_weights(w_in, w_uq, w_ukv):
    bf = jnp.bfloat16
    f32 = jnp.float32
    pad_r = HEAD_SLAB - MLA_NOPE_DIM - MLA_ROPE_DIM
    d = w_in.shape[0]
    w_in_t = jnp.swapaxes(w_in, 0, 1).astype(bf)
    wkpe = w_in[:, _O2:_O3]
    slab = lambda w: jnp.concatenate([jnp.zeros((d, MLA_NOPE_DIM), f32), w, jnp.zeros((d, pad_r), f32)], axis=1)
    wkpe_t = jnp.concatenate([slab(wkpe), slab(_rot_half_cols(wkpe))], axis=1).T.astype(bf)

    r = w_uq.shape[0]
    uq = w_uq.reshape(r, MLA_HEADS, MLA_NOPE_DIM + MLA_ROPE_DIM)
    nope, pe = uq[..., :MLA_NOPE_DIM], uq[..., MLA_NOPE_DIM:]
    plain = jnp.concatenate([nope, pe, jnp.zeros((r, MLA_HEADS, pad_r), f32)], axis=-1)
    wuq = jnp.concatenate([plain.reshape(r, -1), _rot_half_cols(pe).reshape(r, -1)], axis=1).astype(bf)

    r = w_ukv.shape[0]
    ukv = w_ukv.reshape(r, MLA_HEADS, MLA_NOPE_DIM + MLA_V_DIM)
    kn, vv = ukv[..., :MLA_NOPE_DIM], ukv[..., MLA_NOPE_DIM:]
    kslab = jnp.concatenate([kn, jnp.zeros((r, MLA_HEADS, HEAD_SLAB - MLA_NOPE_DIM), f32)], axis=-1)
    wukv = jnp.concatenate([kslab.reshape(r, -1), vv.reshape(r, -1)], axis=1).astype(bf)
    return w_in_t, wkpe_t, wuq, wukv


def kernel(x, positions, w_in, b_gate, mla_q_norm, mla_kv_norm, w_uq, w_ukv, w_o_mla, diff_lambda_q1, diff_lambda_k1, diff_lambda_q2, diff_lambda_k2, diff_subln, w_o_diff, w_o, ln1_g, ln1_b, w_up, w_down, ln2_g, ln2_b):
    b, s, d = x.shape
    bf = jnp.bfloat16
    layer = 0
    w_in_t, wkpe_t, wuq, wukv = _prep_weights(w_in[layer], w_uq[layer], w_ukv[layer])

    inv_freq = 1.0 / (ROPE_THETA ** (jnp.arange(0, MLA_ROPE_DIM, 2, dtype=jnp.float32) / MLA_ROPE_DIM))
    x2 = x.reshape(b * s, d)
    rel = positions - positions[:, :1]
    pos_rows = jnp.concatenate([positions.reshape(-1, 1, PROJ_ROWS), rel.reshape(-1, 1, PROJ_ROWS)], axis=1)
    q, k, v, dq, dk, dv, kpos = _proj(x2, pos_rows, inv_freq[:, None],
                                      w_in_t, wkpe_t, wuq, wukv, mla_q_norm[layer][None, :],
                                      mla_kv_norm[layer][None, :])
    r3 = lambda a: a.reshape(b, s, a.shape[-1])
    o_mla = _mla_attention(r3(q), r3(k), r3(v))

    slopes = jnp.asarray(_alibi_slopes(DIFF_HEADS))
    ordered = jnp.all(positions[:, 1:] >= positions[:, :-1], axis=1)
    narrow = (jnp.min(rel, axis=1) >= 0) & (jnp.max(rel, axis=1) < ORDERED_SPAN_LIMIT)
    ok = (ordered & narrow).astype(jnp.int32)
    o_diff = _diff_attention(slopes, ok, r3(dq), r3(dk), r3(dv), r3(kpos), positions.reshape(b, 1, s),
                             diff_lambda_q1[layer][None, :], diff_lambda_k1[layer][None, :],
                             diff_lambda_q2[layer][None, :], diff_lambda_k2[layer][None, :],
                             diff_subln[layer][None, :])

    row = lambda a: a[layer][None, :]
    out = _post(x2, o_mla.reshape(b * s, -1), o_diff.reshape(b * s, -1), w_in_t, row(b_gate),
                w_o_mla[layer].astype(bf), w_o_diff[layer].astype(bf), w_o[layer].astype(bf),
                row(ln1_g), row(ln1_b), w_up[layer].astype(bf), w_down[layer].astype(bf), row(ln2_g), row(ln2_b))
    return out.reshape(b, s, d)
```

```python
import functools
import math

import jax
import jax.numpy as jnp
import numpy as np
from jax import lax
from jax.experimental import pallas as pl
from jax.experimental.pallas import tpu as pltpu

D_MODEL = 1024
MLA_HEADS = 8
MLA_NOPE_DIM = 64
MLA_ROPE_DIM = 32
MLA_V_DIM = 64
MLA_Q_RANK = 384
MLA_KV_RANK = 256
ROPE_THETA = 10000.0
DIFF_HEADS = 8
DIFF_HEAD_DIM = 64
DIFF_V_DIM = 128
D_FF = 4 * D_MODEL
DEPTH = 1
LN_EPS = 1e-5
RMS_EPS = 1e-6
NEG_INF = -1e30
DEEPNORM_ALPHA = (2.0 * DEPTH) ** 0.25
LAMBDA_INIT = 0.8 - 0.6 * math.exp(-0.3 * 0)
LOG2E = math.log2(math.e)

LANES = 128
HEAD_SLAB = LANES
VMEM_LIMIT = 56 * 1024 * 1024
PIECES = 3
ORDERED_SPAN_LIMIT = 1 << 14

PROJ_ROWS = 512
MLA_TILE = 512
MLA_PAIRS = 4
DIFF_GROUP = 4
DIFF_TILE = 256
POST_ROWS = 512
POST_GROUPS = 2
FF_CHUNK = 1024

_O1 = MLA_Q_RANK
_O2 = _O1 + MLA_KV_RANK
_O3 = _O2 + MLA_ROPE_DIM
_O4 = _O3 + DIFF_HEADS * 2 * DIFF_HEAD_DIM
_O5 = _O4 + DIFF_HEADS * 2 * DIFF_HEAD_DIM
_O6 = _O5 + DIFF_HEADS * DIFF_V_DIM


def _alibi_slopes(n):
    def pow2_slopes(k):
        start = 2.0 ** (-8.0 / k)
        return [start ** (i + 1) for i in range(k)]
    if math.log2(n).is_integer():
        s = pow2_slopes(n)
    else:
        c = 2 ** int(math.floor(math.log2(n)))
        s = pow2_slopes(c) + pow2_slopes(2 * c)[0::2][: n - c]
    return np.asarray(s, dtype=np.float32)


def _resident(shape):
    return pl.BlockSpec(shape, lambda *_: (0,) * len(shape), pipeline_mode=pl.Buffered(1))


def _resident_rows(start, stop, width):
    return pl.BlockSpec((pl.Element(stop - start), pl.Element(width)), lambda *_: (start, 0),
                        pipeline_mode=pl.Buffered(1))


def _rms(x, g):
    ms = jnp.mean(x * x, axis=-1, keepdims=True)
    return x * lax.rsqrt(ms + RMS_EPS) * g


def _layer_norm(x, g, b):
    mu = jnp.mean(x, axis=-1, keepdims=True)
    xc = x - mu
    var = jnp.mean(xc * xc, axis=-1, keepdims=True)
    return xc * lax.rsqrt(var + LN_EPS) * g + b


def _dot(a, b):
    return jnp.dot(a, b, preferred_element_type=jnp.float32)


def _dot_nt(a, b):
    return lax.dot_general(a, b, (((1,), (1,)), ((), ())), preferred_element_type=jnp.float32)


def _bf16_pieces(x):
    out = []
    for _ in range(PIECES):
        p = x.astype(jnp.bfloat16).astype(jnp.float32)
        out.append(p)
        x = x - p
    return out


def _piece_pattern(pieces, by_group, axis, shape):
    idx = lax.broadcasted_iota(jnp.int32, shape, axis)
    out = jnp.zeros(shape, jnp.float32)
    for a in range(PIECES):
        for b in range(PIECES):
            out = jnp.where(idx == PIECES * a + b, pieces[a if by_group else b], out)
    return out


def _token_tables(pos_row, rel_row, invf_col, scale):
    rows = pos_row.shape[1]
    half_r = MLA_ROPE_DIM // 2
    ang = invf_col * pos_row
    cos, sin = jnp.cos(ang), jnp.sin(ang)
    pad = jnp.zeros((HEAD_SLAB - MLA_NOPE_DIM - MLA_ROPE_DIM, rows), jnp.float32)
    nope0 = jnp.zeros((MLA_NOPE_DIM, rows), jnp.float32)
    slab = lambda nope, t: jnp.concatenate([nope, t, t, pad], axis=0).T
    pieces = _piece_pattern(_bf16_pieces(rel_row), False, 0, (half_r, rows))
    kpos = jnp.concatenate([pieces, jnp.zeros((LANES - half_r, rows), jnp.float32)], axis=0).T
    return (slab(nope0 + scale, cos * scale), slab(nope0, sin * scale), slab(nope0, cos), slab(nope0, sin), kpos)


def _proj_kernel(x_ref, pos_ref, invf_ref, wlat_ref, wkpe_ref, wdq_ref, wdk_ref, wdv_ref, wuq_ref, wukv_ref,
                 gq_ref, gkv_ref, q_ref, k_ref, v_ref, dq_ref, dk_ref, dv_ref, kpos_ref):
    xb = x_ref[...].astype(jnp.bfloat16)
    scale = (MLA_NOPE_DIM + MLA_ROPE_DIM) ** -0.5 * LOG2E
    pos = pos_ref[0].astype(jnp.float32)
    cos_s, sin_s, cos, sin, kpos = _token_tables(pos[0:1], pos[1:2], invf_ref[...], scale)
    kpos_ref[...] = kpos.astype(kpos_ref.dtype)

    lat = _dot_nt(xb, wlat_ref[...])

    cq = _rms(lat[:, :_O1], gq_ref[...]).astype(jnp.bfloat16)
    qq = _dot(cq, wuq_ref[...])
    half = MLA_HEADS * HEAD_SLAB
    per_group = LANES // MLA_ROPE_DIM
    for h in range(MLA_HEADS):
        lo = h * HEAD_SLAB
        grp = half + (h // per_group) * LANES
        rot = pltpu.roll(qq[:, grp:grp + LANES], (MLA_NOPE_DIM - (h % per_group) * MLA_ROPE_DIM) % LANES, 1)
        q_ref[:, lo:lo + HEAD_SLAB] = (qq[:, lo:lo + HEAD_SLAB] * cos_s + rot * sin_s).astype(q_ref.dtype)

    kp = _dot_nt(xb, wkpe_ref[...])
    kpe = kp[:, :HEAD_SLAB] * cos + kp[:, HEAD_SLAB:] * sin

    ckv = _rms(lat[:, _O1:_O2], gkv_ref[...]).astype(jnp.bfloat16)
    kv = _dot(ckv, wukv_ref[...])
    for h in range(MLA_HEADS):
        lo = h * HEAD_SLAB
        k_ref[:, lo:lo + HEAD_SLAB] = (kv[:, lo:lo + HEAD_SLAB] + kpe).astype(k_ref.dtype)
    v_ref[...] = kv[:, half:].astype(v_ref.dtype)

    dq_ref[...] = (_dot_nt(xb, wdq_ref[...]) * (DIFF_HEAD_DIM ** -0.5 * LOG2E)).astype(dq_ref.dtype)
    dk_ref[...] = _dot_nt(xb, wdk_ref[...]).astype(dk_ref.dtype)
    dv_ref[...] = _dot_nt(xb, wdv_ref[...]).astype(dv_ref.dtype)


def _proj(x2, pos_rows, invf, w_in_t, wkpe_t, wuq, wukv, gq, gkv):
    t, d = x2.shape
    rows = PROJ_ROWS
    bf = jnp.bfloat16
    tok = lambda w: pl.BlockSpec((rows, w), lambda i: (i, 0))
    wrows = lambda a, b: _resident_rows(a, b, d)
    out_shape = [jax.ShapeDtypeStruct((t, MLA_HEADS * HEAD_SLAB), bf),
                 jax.ShapeDtypeStruct((t, MLA_HEADS * HEAD_SLAB), bf),
                 jax.ShapeDtypeStruct((t, MLA_HEADS * MLA_V_DIM), bf),
                 jax.ShapeDtypeStruct((t, D_MODEL), bf),
                 jax.ShapeDtypeStruct((t, D_MODEL), bf),
                 jax.ShapeDtypeStruct((t, D_MODEL), bf),
                 jax.ShapeDtypeStruct((t, LANES), bf)]
    return pl.pallas_call(
        _proj_kernel,
        grid=(t // rows,),
        in_specs=[tok(D_MODEL), pl.BlockSpec((1, 2, rows), lambda i: (i, 0, 0)), _resident(invf.shape),
                  wrows(0, _O2), _resident(wkpe_t.shape), wrows(_O3, _O4), wrows(_O4, _O5), wrows(_O5, _O6),
                  _resident(wuq.shape), _resident(wukv.shape), _resident(gq.shape), _resident(gkv.shape)],
        out_specs=[tok(s.shape[1]) for s in out_shape],
        out_shape=out_shape,
        compiler_params=pltpu.CompilerParams(dimension_semantics=("parallel",), vmem_limit_bytes=VMEM_LIMIT),
        name="proj",
    )(x2, pos_rows, invf, w_in_t, wkpe_t, w_in_t, w_in_t, w_in_t, wuq, wukv, gq, gkv)


def _tile_order(n):
    return list(range(n - 1, -1, -1))


def _causal_mask(tq, tk):
    row = lax.broadcasted_iota(jnp.int32, (tq, tk), 0)
    col = lax.broadcasted_iota(jnp.int32, (tq, tk), 1)
    return col <= row


def _tile_attention(scores, vaug_ref, r0, mask):
    t = mask.shape[1]
    pv = lambda s, m, lo, hi: _dot(jnp.exp2(s - m).astype(jnp.bfloat16), vaug_ref[lo:hi, :])
    s_d = jnp.where(mask, scores(r0, r0 + t), NEG_INF)
    m = jnp.max(s_d, axis=-1, keepdims=True)
    if r0 > 0:
        s_p = scores(0, r0)
        m = jnp.maximum(m, jnp.max(s_p, axis=-1, keepdims=True))
    acc = pv(s_d, m, r0, r0 + t)
    if r0 > 0:
        acc = acc + pv(s_p, m, 0, r0)
    return acc[:, :LANES] / acc[:, LANES:]


def _mla_kernel(q_ref, k_ref, v_ref, o_ref, vaug_ref):
    tq = MLA_TILE
    seq = q_ref.shape[1]
    for p in range(MLA_PAIRS):
        vaug_ref[p, :, :LANES] = v_ref[0, :, p * LANES:(p + 1) * LANES]
        vaug_ref[p, :, LANES:] = jnp.ones((seq, LANES), vaug_ref.dtype)
    mask = _causal_mask(tq, tq)
    lane = lax.broadcasted_iota(jnp.int32, (tq, 2 * MLA_V_DIM), 1)
    for i in _tile_order(seq // tq):
        r0 = i * tq
        for p in range(MLA_PAIRS):
            outs = []
            for hh in range(2):
                lo = (2 * p + hh) * HEAD_SLAB
                q = q_ref[0, r0:r0 + tq, lo:lo + HEAD_SLAB]
                scores = lambda a, b, q=q, lo=lo: _dot_nt(q, k_ref[0, a:b, lo:lo + HEAD_SLAB])
                outs.append(_tile_attention(scores, vaug_ref.at[p], r0, mask))
            o_ref[0, r0:r0 + tq, p * LANES:(p + 1) * LANES] = jnp.where(lane < MLA_V_DIM, outs[0],
                                                                        outs[1]).astype(o_ref.dtype)


def _mla_attention(q, k, v):
    b, s, _ = q.shape
    steps = MLA_HEADS // (2 * MLA_PAIRS)
    blk = lambda w: pl.BlockSpec((1, s, MLA_PAIRS * w), lambda bi, p: (bi, 0, p))
    return pl.pallas_call(
        _mla_kernel,
        grid=(b, steps),
        in_specs=[blk(2 * HEAD_SLAB), blk(2 * HEAD_SLAB), blk(2 * MLA_V_DIM)],
        out_specs=blk(2 * MLA_V_DIM),
        out_shape=jax.ShapeDtypeStruct((b, s, MLA_HEADS * MLA_V_DIM), jnp.bfloat16),
        scratch_shapes=[pltpu.VMEM((MLA_PAIRS, s, 2 * LANES), jnp.bfloat16)],
        compiler_params=pltpu.CompilerParams(dimension_semantics=("parallel", "parallel"),
                                             vmem_limit_bytes=VMEM_LIMIT),
        name="mla_attn",
    )(q, k, v)


def _diff_kernel(slopes_ref, ok_ref, q_ref, k_ref, v_ref, kpos_ref, pk_ref, lq1_ref, lk1_ref, lq2_ref,
                 lk2_ref, g_ref, o_ref, vaug_ref, kaug_ref):
    tq = DIFF_TILE
    seq = q_ref.shape[1]
    head_lanes = lambda j: slice(j * LANES, (j + 1) * LANES)
    for j in range(DIFF_GROUP):
        vaug_ref[j, :, :LANES] = v_ref[0, :, head_lanes(j)]
        vaug_ref[j, :, LANES:] = jnp.ones((seq, LANES), vaug_ref.dtype)
    both = lambda a: jnp.concatenate([a, a], axis=0)
    mask = both(_causal_mask(tq, tq))
    slopes = [slopes_ref[pl.program_id(1) * DIFF_GROUP + j] * LOG2E for j in range(DIFF_GROUP)]
    lane = lax.broadcasted_iota(jnp.int32, (tq, LANES), 1)
    lam = (jnp.exp(jnp.sum(lq1_ref[...] * lk1_ref[...], axis=-1, keepdims=True))
           - jnp.exp(jnp.sum(lq2_ref[...] * lk2_ref[...], axis=-1, keepdims=True)) + LAMBDA_INIT)

    def tiles(fast):
        coefs = []
        if fast:
            for j in range(DIFF_GROUP):
                kaug_ref[j, :, :LANES] = k_ref[0, :, head_lanes(j)]
                kaug_ref[j, :, LANES:] = kpos_ref[0]
                coef = _piece_pattern(_bf16_pieces(jnp.full((1, 1), slopes[j], jnp.float32)), True, 1, (1, LANES))
                coefs.append(jnp.broadcast_to(coef.astype(jnp.bfloat16), (2 * tq, LANES)))
        for j in range(DIFF_GROUP):
            for i in _tile_order(seq // tq):
                r0 = i * tq
                q = q_ref[0, r0:r0 + tq, head_lanes(j)]
                zero = jnp.zeros_like(q)
                qq = jnp.concatenate([jnp.where(lane < DIFF_HEAD_DIM, q, zero),
                                      jnp.where(lane < DIFF_HEAD_DIM, zero, q)], axis=0)
                if fast:
                    qq = jnp.concatenate([qq, coefs[j]], axis=1)
                    scores = lambda a, b, qq=qq, j=j: _dot_nt(qq, kaug_ref[j, a:b, :])
                else:
                    pq = jnp.broadcast_to(pk_ref[0, :, r0:r0 + tq], (LANES, tq)).T[:, :1]

                    def scores(a, b, qq=qq, pq=pq, j=j):
                        dist = jnp.abs(pq - pk_ref[0, :, a:b]).astype(jnp.float32)
                        return _dot_nt(qq, k_ref[0, a:b, head_lanes(j)]) - both(dist * slopes[j])

                out = _tile_attention(scores, vaug_ref.at[j], r0, mask)
                o = out[:tq] - lam * out[tq:]
                o_ref[0, r0:r0 + tq, head_lanes(j)] = (_rms(o, g_ref[...])
                                                       * (1.0 - LAMBDA_INIT)).astype(o_ref.dtype)

    ok = ok_ref[pl.program_id(0)]
    pl.when(ok == 1)(functools.partial(tiles, True))
    pl.when(ok != 1)(functools.partial(tiles, False))


def _diff_attention(slopes, ok, q, k, v, kpos, pos_row, lq1, lk1, lq2, lk2, g):
    b, s, _ = q.shape
    vec = lambda a: pl.BlockSpec(a.shape, lambda bi, h: (0, 0))
    head = pl.BlockSpec((1, s, DIFF_GROUP * LANES), lambda bi, h: (bi, 0, h))
    smem = pl.BlockSpec(memory_space=pltpu.SMEM)
    return pl.pallas_call(
        _diff_kernel,
        grid=(b, DIFF_HEADS // DIFF_GROUP),
        in_specs=[smem, smem, head, head, head,
                  pl.BlockSpec((1, s, LANES), lambda bi, h: (bi, 0, 0)),
                  pl.BlockSpec((1, 1, s), lambda bi, h: (bi, 0, 0)),
                  vec(lq1), vec(lk1), vec(lq2), vec(lk2), vec(g)],
        out_specs=head,
        out_shape=jax.ShapeDtypeStruct((b, s, DIFF_HEADS * DIFF_V_DIM), jnp.bfloat16),
        scratch_shapes=[pltpu.VMEM((DIFF_GROUP, s, 2 * LANES), jnp.bfloat16)] * 2,
        compiler_params=pltpu.CompilerParams(dimension_semantics=("parallel", "parallel"),
                                             vmem_limit_bytes=VMEM_LIMIT),
        name="diff_attn",
    )(slopes, ok, q, k, v, kpos, pos_row, lq1, lk1, lq2, lk2, g)


def _post_kernel(x_ref, om_ref, od_ref, wg_ref, bg_ref, wom_ref, wod_ref, wo_ref, g1_ref, b1_ref,
                 wup_ref, wdn_ref, g2_ref, b2_ref, o_ref):
    def mix(rows):
        x = x_ref[rows, :]
        xb = x.astype(jnp.bfloat16)
        gate = jax.nn.sigmoid(_dot_nt(xb, wg_ref[...]) + bg_ref[...])
        y_mla = _dot(om_ref[rows, :], wom_ref[...])
        y_diff = _dot(od_ref[rows, :], wod_ref[...])
        merged = gate[:, :D_MODEL] * y_mla + gate[:, D_MODEL:] * y_diff
        mixed = _dot(merged.astype(jnp.bfloat16), wo_ref[...])
        return _layer_norm(DEEPNORM_ALPHA * x + mixed, g1_ref[...], b1_ref[...])

    def mlp(rows, x1):
        x1b = x1.astype(jnp.bfloat16)
        acc = DEEPNORM_ALPHA * x1
        for c in range(D_FF // FF_CHUNK):
            lo = c * FF_CHUNK
            hid = jnp.maximum(_dot(x1b, wup_ref[:, lo:lo + FF_CHUNK]), 0.0)
            acc = acc + _dot((hid * hid).astype(jnp.bfloat16), wdn_ref[lo:lo + FF_CHUNK, :])
        o_ref[rows, :] = _layer_norm(acc, g2_ref[...], b2_ref[...]).astype(o_ref.dtype)

    sub = x_ref.shape[0] // POST_GROUPS
    groups = [slice(g * sub, (g + 1) * sub) for g in range(POST_GROUPS)]
    x1s = [mix(rows) for rows in groups]
    for rows, x1 in zip(groups, x1s):
        mlp(rows, x1)


def _post(x2, o_mla, o_diff, w_in_t, bg, wom, wod, wo, g1, b1, wup, wdn, g2, b2):
    t, d = x2.shape
    rows = POST_ROWS
    tok = lambda w: pl.BlockSpec((rows, w), lambda i: (i, 0))
    consts = (bg, wom, wod, wo, g1, b1, wup, wdn, g2, b2)
    return pl.pallas_call(
        _post_kernel,
        grid=(t // rows,),
        in_specs=([tok(D_MODEL), tok(o_mla.shape[1]), tok(o_diff.shape[1]), _resident_rows(_O6, w_in_t.shape[0], d)]
                  + [_resident(a.shape) for a in consts]),
        out_specs=tok(D_MODEL),
        out_shape=jax.ShapeDtypeStruct((t, D_MODEL), jnp.float32),
        compiler_params=pltpu.CompilerParams(dimension_semantics=("parallel",), vmem_limit_bytes=VMEM_LIMIT),
        name="post",
    )(x2, o_mla, o_diff, w_in_t, *consts)


def _rot_half_cols(w):
    half = w.shape[-1] // 2
    return jnp.concatenate([-w[..., half:], w[..., :half]], axis=-1)


def _prep_weights(w_in, w_uq, w_ukv):
    bf = jnp.bfloat16
    f32 = jnp.float32
    pad_r = HEAD_SLAB - MLA_NOPE_DIM - MLA_ROPE_DIM
    d = w_in.shape[0]
    w_in_t = jnp.swapaxes(w_in, 0, 1).astype(bf)
    wkpe = w_in[:, _O2:_O3]
    slab = lambda w: jnp.concatenate([jnp.zeros((d, MLA_NOPE_DIM), f32), w, jnp.zeros((d, pad_r), f32)], axis=1)
    wkpe_t = jnp.concatenate([slab(wkpe), slab(_rot_half_cols(wkpe))], axis=1).T.astype(bf)

    r = w_uq.shape[0]
    uq = w_uq.reshape(r, MLA_HEADS, MLA_NOPE_DIM + MLA_ROPE_DIM)
    nope, pe = uq[..., :MLA_NOPE_DIM], uq[..., MLA_NOPE_DIM:]
    plain = jnp.concatenate([nope, pe, jnp.zeros((r, MLA_HEADS, pad_r), f32)], axis=-1)
    wuq = jnp.concatenate([plain.reshape(r, -1), _rot_half_cols(pe).reshape(r, -1)], axis=1).astype(bf)

    r = w_ukv.shape[0]
    ukv = w_ukv.reshape(r, MLA_HEADS, MLA_NOPE_DIM + MLA_V_DIM)
    kn, vv = ukv[..., :MLA_NOPE_DIM], ukv[..., MLA_NOPE_DIM:]
    kslab = jnp.concatenate([kn, jnp.zeros((r, MLA_HEADS, HEAD_SLAB - MLA_NOPE_DIM), f32)], axis=-1)
    wukv = jnp.concatenate([kslab.reshape(r, -1), vv.reshape(r, -1)], axis=1).astype(bf)
    return w_in_t, wkpe_t, wuq, wukv


def kernel(x, positions, w_in, b_gate, mla_q_norm, mla_kv_norm, w_uq, w_ukv, w_o_mla, diff_lambda_q1, diff_lambda_k1, diff_lambda_q2, diff_lambda_k2, diff_subln, w_o_diff, w_o, ln1_g, ln1_b, w_up, w_down, ln2_g, ln2_b):
    b, s, d = x.shape
    bf = jnp.bfloat16
    layer = 0
    w_in_t, wkpe_t, wuq, wukv = _prep_weights(w_in[layer], w_uq[layer], w_ukv[layer])

    inv_freq = 1.0 / (ROPE_THETA ** (jnp.arange(0, MLA_ROPE_DIM, 2, dtype=jnp.float32) / MLA_ROPE_DIM))
    x2 = x.reshape(b * s, d)
    rel = positions - positions[:, :1]
    pos_rows = jnp.concatenate([positions.reshape(-1, 1, PROJ_ROWS), rel.reshape(-1, 1, PROJ_ROWS)], axis=1)
    q, k, v, dq, dk, dv, kpos = _proj(x2, pos_rows, inv_freq[:, None],
                                      w_in_t, wkpe_t, wuq, wukv, mla_q_norm[layer][None, :],
                                      mla_kv_norm[layer][None, :])
    r3 = lambda a: a.reshape(b, s, a.shape[-1])
    o_mla = _mla_attention(r3(q), r3(k), r3(v))

    slopes = jnp.asarray(_alibi_slopes(DIFF_HEADS))
    ordered = jnp.all(positions[:, 1:] >= positions[:, :-1], axis=1)
    narrow = (jnp.min(rel, axis=1) >= 0) & (jnp.max(rel, axis=1) < ORDERED_SPAN_LIMIT)
    ok = (ordered & narrow).astype(jnp.int32)
    o_diff = _diff_attention(slopes, ok, r3(dq), r3(dk), r3(dv), r3(kpos), positions.reshape(b, 1, s),
                             diff_lambda_q1[layer][None, :], diff_lambda_k1[layer][None, :],
                             diff_lambda_q2[layer][None, :], diff_lambda_k2[layer][None, :],
                             diff_subln[layer][None, :])

    row = lambda a: a[layer][None, :]
    out = _post(x2, o_mla.reshape(b * s, -1), o_diff.reshape(b * s, -1), w_in_t, row(b_gate),
                w_o_mla[layer].astype(bf), w_o_diff[layer].astype(bf), w_o[layer].astype(bf),
                row(ln1_g), row(ln1_b), w_up[layer].astype(bf), w_down[layer].astype(bf), row(ln2_g), row(ln2_b))
    return out.reshape(b, s, d)
```

```python
import functools
import math

import jax
import jax.numpy as jnp
import numpy as np
from jax import lax
from jax.experimental import pallas as pl
from jax.experimental.pallas import tpu as pltpu

D_MODEL = 1024
MLA_HEADS = 8
MLA_NOPE_DIM = 64
MLA_ROPE_DIM = 32
MLA_V_DIM = 64
MLA_Q_RANK = 384
MLA_KV_RANK = 256
ROPE_THETA = 10000.0
DIFF_HEADS = 8
DIFF_HEAD_DIM = 64
DIFF_V_DIM = 128
D_FF = 4 * D_MODEL
DEPTH = 1
LN_EPS = 1e-5
RMS_EPS = 1e-6
NEG_INF = -1e30
DEEPNORM_ALPHA = (2.0 * DEPTH) ** 0.25
LAMBDA_INIT = 0.8 - 0.6 * math.exp(-0.3 * 0)
LOG2E = math.log2(math.e)

LANES = 128
HEAD_SLAB = LANES
VMEM_LIMIT = 56 * 1024 * 1024
PIECES = 3
ORDERED_SPAN_LIMIT = 1 << 14

PROJ_ROWS = 512
MLA_TILE = 512
MLA_PAIRS = 4
DIFF_GROUP = 2
DIFF_TILE = 256
POST_ROWS = 512
POST_GROUPS = 2
FF_CHUNK = 1024

_O1 = MLA_Q_RANK
_O2 = _O1 + MLA_KV_RANK
_O3 = _O2 + MLA_ROPE_DIM
_O4 = _O3 + DIFF_HEADS * 2 * DIFF_HEAD_DIM
_O5 = _O4 + DIFF_HEADS * 2 * DIFF_HEAD_DIM
_O6 = _O5 + DIFF_HEADS * DIFF_V_DIM


def _alibi_slopes(n):
    def pow2_slopes(k):
        start = 2.0 ** (-8.0 / k)
        return [start ** (i + 1) for i in range(k)]
    if math.log2(n).is_integer():
        s = pow2_slopes(n)
    else:
        c = 2 ** int(math.floor(math.log2(n)))
        s = pow2_slopes(c) + pow2_slopes(2 * c)[0::2][: n - c]
    return np.asarray(s, dtype=np.float32)


def _resident(shape):
    return pl.BlockSpec(shape, lambda *_: (0,) * len(shape), pipeline_mode=pl.Buffered(1))


def _resident_rows(start, stop, width):
    return pl.BlockSpec((pl.Element(stop - start), pl.Element(width)), lambda *_: (start, 0),
                        pipeline_mode=pl.Buffered(1))


def _rms(x, g):
    ms = jnp.mean(x * x, axis=-1, keepdims=True)
    return x * lax.rsqrt(ms + RMS_EPS) * g


def _layer_norm(x, g, b):
    mu = jnp.mean(x, axis=-1, keepdims=True)
    xc = x - mu
    var = jnp.mean(xc * xc, axis=-1, keepdims=True)
    return xc * lax.rsqrt(var + LN_EPS) * g + b


def _dot(a, b):
    return jnp.dot(a, b, preferred_element_type=jnp.float32)


def _dot_nt(a, b):
    return lax.dot_general(a, b, (((1,), (1,)), ((), ())), preferred_element_type=jnp.float32)


def _bf16_pieces(x):
    out = []
    for _ in range(PIECES):
        p = x.astype(jnp.bfloat16).astype(jnp.float32)
        out.append(p)
        x = x - p
    return out


def _piece_pattern(pieces, by_group, axis, shape):
    idx = lax.broadcasted_iota(jnp.int32, shape, axis)
    out = jnp.zeros(shape, jnp.float32)
    for a in range(PIECES):
        for b in range(PIECES):
            out = jnp.where(idx == PIECES * a + b, pieces[a if by_group else b], out)
    return out


def _token_tables(pos_row, rel_row, invf_col, scale):
    rows = pos_row.shape[1]
    half_r = MLA_ROPE_DIM // 2
    ang = invf_col * pos_row
    cos, sin = jnp.cos(ang), jnp.sin(ang)
    pad = jnp.zeros((HEAD_SLAB - MLA_NOPE_DIM - MLA_ROPE_DIM, rows), jnp.float32)
    nope0 = jnp.zeros((MLA_NOPE_DIM, rows), jnp.float32)
    slab = lambda nope, t: jnp.concatenate([nope, t, t, pad], axis=0).T
    pieces = _piece_pattern(_bf16_pieces(rel_row), False, 0, (half_r, rows))
    kpos = jnp.concatenate([pieces, jnp.zeros((LANES - half_r, rows), jnp.float32)], axis=0).T
    return (slab(nope0 + scale, cos * scale), slab(nope0, sin * scale), slab(nope0, cos), slab(nope0, sin), kpos)


def _proj_kernel(x_ref, pos_ref, invf_ref, wlat_ref, wkpe_ref, wdq_ref, wdk_ref, wdv_ref, wuq_ref, wukv_ref,
                 gq_ref, gkv_ref, q_ref, k_ref, v_ref, dq_ref, dk_ref, dv_ref, kpos_ref):
    xb = x_ref[...].astype(jnp.bfloat16)
    scale = (MLA_NOPE_DIM + MLA_ROPE_DIM) ** -0.5 * LOG2E
    pos = pos_ref[0].astype(jnp.float32)
    cos_s, sin_s, cos, sin, kpos = _token_tables(pos[0:1], pos[1:2], invf_ref[...], scale)
    kpos_ref[...] = kpos.astype(kpos_ref.dtype)

    lat = _dot_nt(xb, wlat_ref[...])

    cq = _rms(lat[:, :_O1], gq_ref[...]).astype(jnp.bfloat16)
    qq = _dot(cq, wuq_ref[...])
    half = MLA_HEADS * HEAD_SLAB
    per_group = LANES // MLA_ROPE_DIM
    for h in range(MLA_HEADS):
        lo = h * HEAD_SLAB
        grp = half + (h // per_group) * LANES
        rot = pltpu.roll(qq[:, grp:grp + LANES], (MLA_NOPE_DIM - (h % per_group) * MLA_ROPE_DIM) % LANES, 1)
        q_ref[:, lo:lo + HEAD_SLAB] = (qq[:, lo:lo + HEAD_SLAB] * cos_s + rot * sin_s).astype(q_ref.dtype)

    kp = _dot_nt(xb, wkpe_ref[...])
    kpe = kp[:, :HEAD_SLAB] * cos + kp[:, HEAD_SLAB:] * sin

    ckv = _rms(lat[:, _O1:_O2], gkv_ref[...]).astype(jnp.bfloat16)
    kv = _dot(ckv, wukv_ref[...])
    for h in range(MLA_HEADS):
        lo = h * HEAD_SLAB
        k_ref[:, lo:lo + HEAD_SLAB] = (kv[:, lo:lo + HEAD_SLAB] + kpe).astype(k_ref.dtype)
    v_ref[...] = kv[:, half:].astype(v_ref.dtype)

    dq_ref[...] = (_dot_nt(xb, wdq_ref[...]) * (DIFF_HEAD_DIM ** -0.5 * LOG2E)).astype(dq_ref.dtype)
    dk_ref[...] = _dot_nt(xb, wdk_ref[...]).astype(dk_ref.dtype)
    dv_ref[...] = _dot_nt(xb, wdv_ref[...]).astype(dv_ref.dtype)


def _proj(x2, pos_rows, invf, w_in_t, wkpe_t, wuq, wukv, gq, gkv):
    t, d = x2.shape
    rows = PROJ_ROWS
    bf = jnp.bfloat16
    tok = lambda w: pl.BlockSpec((rows, w), lambda i: (i, 0))
    wrows = lambda a, b: _resident_rows(a, b, d)
    out_shape = [jax.ShapeDtypeStruct((t, MLA_HEADS * HEAD_SLAB), bf),
                 jax.ShapeDtypeStruct((t, MLA_HEADS * HEAD_SLAB), bf),
                 jax.ShapeDtypeStruct((t, MLA_HEADS * MLA_V_DIM), bf),
                 jax.ShapeDtypeStruct((t, D_MODEL), bf),
                 jax.ShapeDtypeStruct((t, D_MODEL), bf),
                 jax.ShapeDtypeStruct((t, D_MODEL), bf),
                 jax.ShapeDtypeStruct((t, LANES), bf)]
    return pl.pallas_call(
        _proj_kernel,
        grid=(t // rows,),
        in_specs=[tok(D_MODEL), pl.BlockSpec((1, 2, rows), lambda i: (i, 0, 0)), _resident(invf.shape),
                  wrows(0, _O2), _resident(wkpe_t.shape), wrows(_O3, _O4), wrows(_O4, _O5), wrows(_O5, _O6),
                  _resident(wuq.shape), _resident(wukv.shape), _resident(gq.shape), _resident(gkv.shape)],
        out_specs=[tok(s.shape[1]) for s in out_shape],
        out_shape=out_shape,
        compiler_params=pltpu.CompilerParams(dimension_semantics=("parallel",), vmem_limit_bytes=VMEM_LIMIT),
        name="proj",
    )(x2, pos_rows, invf, w_in_t, wkpe_t, w_in_t, w_in_t, w_in_t, wuq, wukv, gq, gkv)


def _tile_order(n):
    return list(range(n - 1, -1, -1))


def _causal_mask(tq, tk):
    row = lax.broadcasted_iota(jnp.int32, (tq, tk), 0)
    col = lax.broadcasted_iota(jnp.int32, (tq, tk), 1)
    return col <= row


def _tile_attention(scores, vaug_ref, r0, mask):
    t = mask.shape[1]
    pv = lambda s, m, lo, hi: _dot(jnp.exp2(s - m).astype(jnp.bfloat16), vaug_ref[lo:hi, :])
    s_d = jnp.where(mask, scores(r0, r0 + t), NEG_INF)
    m = jnp.max(s_d, axis=-1, keepdims=True)
    if r0 > 0:
        s_p = scores(0, r0)
        m = jnp.maximum(m, jnp.max(s_p, axis=-1, keepdims=True))
    acc = pv(s_d, m, r0, r0 + t)
    if r0 > 0:
        acc = acc + pv(s_p, m, 0, r0)
    return acc[:, :LANES] / acc[:, LANES:]


def _mla_kernel(q_ref, k_ref, v_ref, o_ref, vaug_ref):
    tq = MLA_TILE
    seq = q_ref.shape[1]
    for p in range(MLA_PAIRS):
        vaug_ref[p, :, :LANES] = v_ref[0, :, p * LANES:(p + 1) * LANES]
        vaug_ref[p, :, LANES:] = jnp.ones((seq, LANES), vaug_ref.dtype)
    mask = _causal_mask(tq, tq)
    lane = lax.broadcasted_iota(jnp.int32, (tq, 2 * MLA_V_DIM), 1)
    for i in _tile_order(seq // tq):
        r0 = i * tq
        for p in range(MLA_PAIRS):
            outs = []
            for hh in range(2):
                lo = (2 * p + hh) * HEAD_SLAB
                q = q_ref[0, r0:r0 + tq, lo:lo + HEAD_SLAB]
                scores = lambda a, b, q=q, lo=lo: _dot_nt(q, k_ref[0, a:b, lo:lo + HEAD_SLAB])
                outs.append(_tile_attention(scores, vaug_ref.at[p], r0, mask))
            o_ref[0, r0:r0 + tq, p * LANES:(p + 1) * LANES] = jnp.where(lane < MLA_V_DIM, outs[0],
                                                                        outs[1]).astype(o_ref.dtype)


def _mla_attention(q, k, v):
    b, s, _ = q.shape
    steps = MLA_HEADS // (2 * MLA_PAIRS)
    blk = lambda w: pl.BlockSpec((1, s, MLA_PAIRS * w), lambda bi, p: (bi, 0, p))
    return pl.pallas_call(
        _mla_kernel,
        grid=(b, steps),
        in_specs=[blk(2 * HEAD_SLAB), blk(2 * HEAD_SLAB), blk(2 * MLA_V_DIM)],
        out_specs=blk(2 * MLA_V_DIM),
        out_shape=jax.ShapeDtypeStruct((b, s, MLA_HEADS * MLA_V_DIM), jnp.bfloat16),
        scratch_shapes=[pltpu.VMEM((MLA_PAIRS, s, 2 * LANES), jnp.bfloat16)],
        compiler_params=pltpu.CompilerParams(dimension_semantics=("parallel", "parallel"),
                                             vmem_limit_bytes=VMEM_LIMIT),
        name="mla_attn",
    )(q, k, v)


def _diff_kernel(slopes_ref, ok_ref, q_ref, k_ref, v_ref, kpos_ref, pk_ref, lq1_ref, lk1_ref, lq2_ref,
                 lk2_ref, g_ref, o_ref, vaug_ref, kaug_ref):
    tq = DIFF_TILE
    seq = q_ref.shape[1]
    head_lanes = lambda j: slice(j * LANES, (j + 1) * LANES)
    for j in range(DIFF_GROUP):
        vaug_ref[j, :, :LANES] = v_ref[0, :, head_lanes(j)]
        vaug_ref[j, :, LANES:] = jnp.ones((seq, LANES), vaug_ref.dtype)
    both = lambda a: jnp.concatenate([a, a], axis=0)
    mask = both(_causal_mask(tq, tq))
    slopes = [slopes_ref[pl.program_id(1) * DIFF_GROUP + j] * LOG2E for j in range(DIFF_GROUP)]
    lane = lax.broadcasted_iota(jnp.int32, (tq, LANES), 1)
    lam = (jnp.exp(jnp.sum(lq1_ref[...] * lk1_ref[...], axis=-1, keepdims=True))
           - jnp.exp(jnp.sum(lq2_ref[...] * lk2_ref[...], axis=-1, keepdims=True)) + LAMBDA_INIT)

    def tiles(fast):
        coefs = []
        if fast:
            for j in range(DIFF_GROUP):
                kaug_ref[j, :, :LANES] = k_ref[0, :, head_lanes(j)]
                kaug_ref[j, :, LANES:] = kpos_ref[0]
                coef = _piece_pattern(_bf16_pieces(jnp.full((1, 1), slopes[j], jnp.float32)), True, 1, (1, LANES))
                coefs.append(jnp.broadcast_to(coef.astype(jnp.bfloat16), (2 * tq, LANES)))
        for j in range(DIFF_GROUP):
            for i in _tile_order(seq // tq):
                r0 = i * tq
                q = q_ref[0, r0:r0 + tq, head_lanes(j)]
                zero = jnp.zeros_like(q)
                qq = jnp.concatenate([jnp.where(lane < DIFF_HEAD_DIM, q, zero),
                                      jnp.where(lane < DIFF_HEAD_DIM, zero, q)], axis=0)
                if fast:
                    qq = jnp.concatenate([qq, coefs[j]], axis=1)
                    scores = lambda a, b, qq=qq, j=j: _dot_nt(qq, kaug_ref[j, a:b, :])
                else:
                    pq = jnp.broadcast_to(pk_ref[0, :, r0:r0 + tq], (LANES, tq)).T[:, :1]

                    def scores(a, b, qq=qq, pq=pq, j=j):
                        dist = jnp.abs(pq - pk_ref[0, :, a:b]).astype(jnp.float32)
                        return _dot_nt(qq, k_ref[0, a:b, head_lanes(j)]) - both(dist * slopes[j])

                out = _tile_attention(scores, vaug_ref.at[j], r0, mask)
                o = out[:tq] - lam * out[tq:]
                o_ref[0, r0:r0 + tq, head_lanes(j)] = (_rms(o, g_ref[...])
                                                       * (1.0 - LAMBDA_INIT)).astype(o_ref.dtype)

    ok = ok_ref[pl.program_id(0)]
    pl.when(ok == 1)(functools.partial(tiles, True))
    pl.when(ok != 1)(functools.partial(tiles, False))


def _diff_attention(slopes, ok, q, k, v, kpos, pos_row, lq1, lk1, lq2, lk2, g):
    b, s, _ = q.shape
    vec = lambda a: pl.BlockSpec(a.shape, lambda bi, h: (0, 0))
    head = pl.BlockSpec((1, s, DIFF_GROUP * LANES), lambda bi, h: (bi, 0, h))
    smem = pl.BlockSpec(memory_space=pltpu.SMEM)
    return pl.pallas_call(
        _diff_kernel,
        grid=(b, DIFF_HEADS // DIFF_GROUP),
        in_specs=[smem, smem, head, head, head,
                  pl.BlockSpec((1, s, LANES), lambda bi, h: (bi, 0, 0)),
                  pl.BlockSpec((1, 1, s), lambda bi, h: (bi, 0, 0)),
                  vec(lq1), vec(lk1), vec(lq2), vec(lk2), vec(g)],
        out_specs=head,
        out_shape=jax.ShapeDtypeStruct((b, s, DIFF_HEADS * DIFF_V_DIM), jnp.bfloat16),
        scratch_shapes=[pltpu.VMEM((DIFF_GROUP, s, 2 * LANES), jnp.bfloat16)] * 2,
        compiler_params=pltpu.CompilerParams(dimension_semantics=("parallel", "parallel"),
                                             vmem_limit_bytes=VMEM_LIMIT),
        name="diff_attn",
    )(slopes, ok, q, k, v, kpos, pos_row, lq1, lk1, lq2, lk2, g)


def _post_kernel(x_ref, om_ref, od_ref, wg_ref, bg_ref, wom_ref, wod_ref, wo_ref, g1_ref, b1_ref,
                 wup_ref, wdn_ref, g2_ref, b2_ref, o_ref):
    def mix(rows):
        x = x_ref[rows, :]
        xb = x.astype(jnp.bfloat16)
        gate = jax.nn.sigmoid(_dot_nt(xb, wg_ref[...]) + bg_ref[...])
        y_mla = _dot(om_ref[rows, :], wom_ref[...])
        y_diff = _dot(od_ref[rows, :], wod_ref[...])
        merged = gate[:, :D_MODEL] * y_mla + gate[:, D_MODEL:] * y_diff
        mixed = _dot(merged.astype(jnp.bfloat16), wo_ref[...])
        return _layer_norm(DEEPNORM_ALPHA * x + mixed, g1_ref[...], b1_ref[...])

    def mlp(rows, x1):
        x1b = x1.astype(jnp.bfloat16)
        acc = DEEPNORM_ALPHA * x1
        for c in range(D_FF // FF_CHUNK):
            lo = c * FF_CHUNK
            hid = jnp.maximum(_dot(x1b, wup_ref[:, lo:lo + FF_CHUNK]), 0.0)
            acc = acc + _dot((hid * hid).astype(jnp.bfloat16), wdn_ref[lo:lo + FF_CHUNK, :])
        o_ref[rows, :] = _layer_norm(acc, g2_ref[...], b2_ref[...]).astype(o_ref.dtype)

    sub = x_ref.shape[0] // POST_GROUPS
    groups = [slice(g * sub, (g + 1) * sub) for g in range(POST_GROUPS)]
    x1s = [mix(rows) for rows in groups]
    for rows, x1 in zip(groups, x1s):
        mlp(rows, x1)


def _post(x2, o_mla, o_diff, w_in_t, bg, wom, wod, wo, g1, b1, wup, wdn, g2, b2):
    t, d = x2.shape
    rows = POST_ROWS
    tok = lambda w: pl.BlockSpec((rows, w), lambda i: (i, 0))
    consts = (bg, wom, wod, wo, g1, b1, wup, wdn, g2, b2)
    return pl.pallas_call(
        _post_kernel,
        grid=(t // rows,),
        in_specs=([tok(D_MODEL), tok(o_mla.shape[1]), tok(o_diff.shape[1]), _resident_rows(_O6, w_in_t.shape[0], d)]
                  + [_resident(a.shape) for a in consts]),
        out_specs=tok(D_MODEL),
        out_shape=jax.ShapeDtypeStruct((t, D_MODEL), jnp.float32),
        compiler_params=pltpu.CompilerParams(dimension_semantics=("parallel",), vmem_limit_bytes=VMEM_LIMIT),
        name="post",
    )(x2, o_mla, o_diff, w_in_t, *consts)


def _rot_half_cols(w):
    half = w.shape[-1] // 2
    return jnp.concatenate([-w[..., half:], w[..., :half]], axis=-1)


def _prep_weights(w_in, w_uq, w_ukv):
    bf = jnp.bfloat16
    f32 = jnp.float32
    pad_r = HEAD_SLAB - MLA_NOPE_DIM - MLA_ROPE_DIM
    d = w_in.shape[0]
    w_in_t = jnp.swapaxes(w_in, 0, 1).astype(bf)
    wkpe = w_in[:, _O2:_O3]
    slab = lambda w: jnp.concatenate([jnp.zeros((d, MLA_NOPE_DIM), f32), w, jnp.zeros((d, pad_r), f32)], axis=1)
    wkpe_t = jnp.concatenate([slab(wkpe), slab(_rot_half_cols(wkpe))], axis=1).T.astype(bf)

    r = w_uq.shape[0]
    uq = w_uq.reshape(r, MLA_HEADS, MLA_NOPE_DIM + MLA_ROPE_DIM)
    nope, pe = uq[..., :MLA_NOPE_DIM], uq[..., MLA_NOPE_DIM:]
    plain = jnp.concatenate([nope, pe, jnp.zeros((r, MLA_HEADS, pad_r), f32)], axis=-1)
    wuq = jnp.concatenate([plain.reshape(r, -1), _rot_half_cols(pe).reshape(r, -1)], axis=1).astype(bf)

    r = w_ukv.shape[0]
    ukv = w_ukv.reshape(r, MLA_HEADS, MLA_NOPE_DIM + MLA_V_DIM)
    kn, vv = ukv[..., :MLA_NOPE_DIM], ukv[..., MLA_NOPE_DIM:]
    kslab = jnp.concatenate([kn, jnp.zeros((r, MLA_HEADS, HEAD_SLAB - MLA_NOPE_DIM), f32)], axis=-1)
    wukv = jnp.concatenate([kslab.reshape(r, -1), vv.reshape(r, -1)], axis=1).astype(bf)
    return w_in_t, wkpe_t, wuq, wukv


def kernel(x, positions, w_in, b_gate, mla_q_norm, mla_kv_norm, w_uq, w_ukv, w_o_mla, diff_lambda_q1, diff_lambda_k1, diff_lambda_q2, diff_lambda_k2, diff_subln, w_o_diff, w_o, ln1_g, ln1_b, w_up, w_down, ln2_g, ln2_b):
    b, s, d = x.shape
    bf = jnp.bfloat16
    layer = 0
    w_in_t, wkpe_t, wuq, wukv = _prep_weights(w_in[layer], w_uq[layer], w_ukv[layer])

    inv_freq = 1.0 / (ROPE_THETA ** (jnp.arange(0, MLA_ROPE_DIM, 2, dtype=jnp.float32) / MLA_ROPE_DIM))
    x2 = x.reshape(b * s, d)
    rel = positions - positions[:, :1]
    pos_rows = jnp.concatenate([positions.reshape(-1, 1, PROJ_ROWS), rel.reshape(-1, 1, PROJ_ROWS)], axis=1)
    q, k, v, dq, dk, dv, kpos = _proj(x2, pos_rows, inv_freq[:, None],
                                      w_in_t, wkpe_t, wuq, wukv, mla_q_norm[layer][None, :],
                                      mla_kv_norm[layer][None, :])
    r3 = lambda a: a.reshape(b, s, a.shape[-1])
    o_mla = _mla_attention(r3(q), r3(k), r3(v))

    slopes = jnp.asarray(_alibi_slopes(DIFF_HEADS))
    ordered = jnp.all(positions[:, 1:] >= positions[:, :-1], axis=1)
    narrow = (jnp.min(rel, axis=1) >= 0) & (jnp.max(rel, axis=1) < ORDERED_SPAN_LIMIT)
    ok = (ordered & narrow).astype(jnp.int32)
    o_diff = _diff_attention(slopes, ok, r3(dq), r3(dk), r3(dv), r3(kpos), positions.reshape(b, 1, s),
                             diff_lambda_q1[layer][None, :], diff_lambda_k1[layer][None, :],
                             diff_lambda_q2[layer][None, :], diff_lambda_k2[layer][None, :],
                             diff_subln[layer][None, :])

    row = lambda a: a[layer][None, :]
    out = _post(x2, o_mla.reshape(b * s, -1), o_diff.reshape(b * s, -1), w_in_t, row(b_gate),
                w_o_mla[layer].astype(bf), w_o_diff[layer].astype(bf), w_o[layer].astype(bf),
                row(ln1_g), row(ln1_b), w_up[layer].astype(bf), w_down[layer].astype(bf), row(ln2_g), row(ln2_b))
    return out.reshape(b, s, d)
```

```python
import functools
import math

import jax
import jax.numpy as jnp
import numpy as np
from jax import lax
from jax.experimental import pallas as pl
from jax.experimental.pallas import tpu as pltpu

D_MODEL = 1024
MLA_HEADS = 8
MLA_NOPE_DIM = 64
MLA_ROPE_DIM = 32
MLA_V_DIM = 64
MLA_Q_RANK = 384
MLA_KV_RANK = 256
ROPE_THETA = 10000.0
DIFF_HEADS = 8
DIFF_HEAD_DIM = 64
DIFF_V_DIM = 128
D_FF = 4 * D_MODEL
DEPTH = 1
LN_EPS = 1e-5
RMS_EPS = 1e-6
NEG_INF = -1e30
DEEPNORM_ALPHA = (2.0 * DEPTH) ** 0.25
LAMBDA_INIT = 0.8 - 0.6 * math.exp(-0.3 * 0)
LOG2E = math.log2(math.e)

LANES = 128
HEAD_SLAB = LANES
VMEM_LIMIT = 56 * 1024 * 1024
PIECES = 3
ORDERED_SPAN_LIMIT = 1 << 14

PROJ_ROWS = 512
MLA_TILE = 512
MLA_PAIRS = 4
DIFF_TILE = 256
POST_ROWS = 512
POST_GROUPS = 2
FF_CHUNK = 1024

_O1 = MLA_Q_RANK
_O2 = _O1 + MLA_KV_RANK
_O3 = _O2 + MLA_ROPE_DIM
_O4 = _O3 + DIFF_HEADS * 2 * DIFF_HEAD_DIM
_O5 = _O4 + DIFF_HEADS * 2 * DIFF_HEAD_DIM
_O6 = _O5 + DIFF_HEADS * DIFF_V_DIM


def _alibi_slopes(n):
    def pow2_slopes(k):
        start = 2.0 ** (-8.0 / k)
        return [start ** (i + 1) for i in range(k)]
    if math.log2(n).is_integer():
        s = pow2_slopes(n)
    else:
        c = 2 ** int(math.floor(math.log2(n)))
        s = pow2_slopes(c) + pow2_slopes(2 * c)[0::2][: n - c]
    return np.asarray(s, dtype=np.float32)


def _resident(shape):
    return pl.BlockSpec(shape, lambda *_: (0,) * len(shape), pipeline_mode=pl.Buffered(1))


def _resident_rows(start, stop, width):
    return pl.BlockSpec((pl.Element(stop - start), pl.Element(width)), lambda *_: (start, 0),
                        pipeline_mode=pl.Buffered(1))


def _rms(x, g):
    ms = jnp.mean(x * x, axis=-1, keepdims=True)
    return x * lax.rsqrt(ms + RMS_EPS) * g


def _layer_norm(x, g, b):
    mu = jnp.mean(x, axis=-1, keepdims=True)
    xc = x - mu
    var = jnp.mean(xc * xc, axis=-1, keepdims=True)
    return xc * lax.rsqrt(var + LN_EPS) * g + b


def _dot(a, b):
    return jnp.dot(a, b, preferred_element_type=jnp.float32)


def _dot_nt(a, b):
    return lax.dot_general(a, b, (((1,), (1,)), ((), ())), preferred_element_type=jnp.float32)


def _bf16_pieces(x):
    out = []
    for _ in range(PIECES):
        p = x.astype(jnp.bfloat16).astype(jnp.float32)
        out.append(p)
        x = x - p
    return out


def _piece_pattern(pieces, by_group, axis, shape):
    idx = lax.broadcasted_iota(jnp.int32, shape, axis)
    out = jnp.zeros(shape, jnp.float32)
    for a in range(PIECES):
        for b in range(PIECES):
            out = jnp.where(idx == PIECES * a + b, pieces[a if by_group else b], out)
    return out


def _token_tables(pos_row, rel_row, invf_col, scale):
    rows = pos_row.shape[1]
    half_r = MLA_ROPE_DIM // 2
    ang = invf_col * pos_row
    cos, sin = jnp.cos(ang), jnp.sin(ang)
    pad = jnp.zeros((HEAD_SLAB - MLA_NOPE_DIM - MLA_ROPE_DIM, rows), jnp.float32)
    nope0 = jnp.zeros((MLA_NOPE_DIM, rows), jnp.float32)
    slab = lambda nope, t: jnp.concatenate([nope, t, t, pad], axis=0).T
    pieces = _piece_pattern(_bf16_pieces(rel_row), False, 0, (half_r, rows))
    kpos = jnp.concatenate([pieces, jnp.zeros((LANES - half_r, rows), jnp.float32)], axis=0).T
    return (slab(nope0 + scale, cos * scale), slab(nope0, sin * scale), slab(nope0, cos), slab(nope0, sin), kpos)


def _proj_kernel(x_ref, pos_ref, invf_ref, wlat_ref, wkpe_ref, wdq_ref, wdk_ref, wdv_ref, wuq_ref, wukv_ref,
                 gq_ref, gkv_ref, q_ref, k_ref, v_ref, dq_ref, dk_ref, dv_ref, kpos_ref):
    xb = x_ref[...].astype(jnp.bfloat16)
    scale = (MLA_NOPE_DIM + MLA_ROPE_DIM) ** -0.5 * LOG2E
    pos = pos_ref[0].astype(jnp.float32)
    cos_s, sin_s, cos, sin, kpos = _token_tables(pos[0:1], pos[1:2], invf_ref[...], scale)
    kpos_ref[...] = kpos.astype(kpos_ref.dtype)

    lat = _dot_nt(xb, wlat_ref[...])

    cq = _rms(lat[:, :_O1], gq_ref[...]).astype(jnp.bfloat16)
    qq = _dot(cq, wuq_ref[...])
    half = MLA_HEADS * HEAD_SLAB
    per_group = LANES // MLA_ROPE_DIM
    for h in range(MLA_HEADS):
        lo = h * HEAD_SLAB
        grp = half + (h // per_group) * LANES
        rot = pltpu.roll(qq[:, grp:grp + LANES], (MLA_NOPE_DIM - (h % per_group) * MLA_ROPE_DIM) % LANES, 1)
        q_ref[:, lo:lo + HEAD_SLAB] = (qq[:, lo:lo + HEAD_SLAB] * cos_s + rot * sin_s).astype(q_ref.dtype)

    kp = _dot_nt(xb, wkpe_ref[...])
    kpe = kp[:, :HEAD_SLAB] * cos + kp[:, HEAD_SLAB:] * sin

    ckv = _rms(lat[:, _O1:_O2], gkv_ref[...]).astype(jnp.bfloat16)
    kv = _dot(ckv, wukv_ref[...])
    for h in range(MLA_HEADS):
        lo = h * HEAD_SLAB
        k_ref[:, lo:lo + HEAD_SLAB] = (kv[:, lo:lo + HEAD_SLAB] + kpe).astype(k_ref.dtype)
    v_ref[...] = kv[:, half:].astype(v_ref.dtype)

    dq_ref[...] = (_dot_nt(xb, wdq_ref[...]) * (DIFF_HEAD_DIM ** -0.5 * LOG2E)).astype(dq_ref.dtype)
    dk_ref[...] = _dot_nt(xb, wdk_ref[...]).astype(dk_ref.dtype)
    dv_ref[...] = _dot_nt(xb, wdv_ref[...]).astype(dv_ref.dtype)


def _proj(x2, pos_rows, invf, w_in_t, wkpe_t, wuq, wukv, gq, gkv):
    t, d = x2.shape
    rows = PROJ_ROWS
    bf = jnp.bfloat16
    tok = lambda w: pl.BlockSpec((rows, w), lambda i: (i, 0))
    wrows = lambda a, b: _resident_rows(a, b, d)
    out_shape = [jax.ShapeDtypeStruct((t, MLA_HEADS * HEAD_SLAB), bf),
                 jax.ShapeDtypeStruct((t, MLA_HEADS * HEAD_SLAB), bf),
                 jax.ShapeDtypeStruct((t, MLA_HEADS * MLA_V_DIM), bf),
                 jax.ShapeDtypeStruct((t, D_MODEL), bf),
                 jax.ShapeDtypeStruct((t, D_MODEL), bf),
                 jax.ShapeDtypeStruct((t, D_MODEL), bf),
                 jax.ShapeDtypeStruct((t, LANES), bf)]
    return pl.pallas_call(
        _proj_kernel,
        grid=(t // rows,),
        in_specs=[tok(D_MODEL), pl.BlockSpec((1, 2, rows), lambda i: (i, 0, 0)), _resident(invf.shape),
                  wrows(0, _O2), _resident(wkpe_t.shape), wrows(_O3, _O4), wrows(_O4, _O5), wrows(_O5, _O6),
                  _resident(wuq.shape), _resident(wukv.shape), _resident(gq.shape), _resident(gkv.shape)],
        out_specs=[tok(s.shape[1]) for s in out_shape],
        out_shape=out_shape,
        compiler_params=pltpu.CompilerParams(dimension_semantics=("parallel",), vmem_limit_bytes=VMEM_LIMIT),
        name="proj",
    )(x2, pos_rows, invf, w_in_t, wkpe_t, w_in_t, w_in_t, w_in_t, wuq, wukv, gq, gkv)


def _tile_order(n):
    return list(range(n - 1, -1, -1))


def _causal_mask(tq, tk):
    row = lax.broadcasted_iota(jnp.int32, (tq, tk), 0)
    col = lax.broadcasted_iota(jnp.int32, (tq, tk), 1)
    return col <= row


def _tile_attention(scores, vaug_ref, r0, mask):
    t = mask.shape[1]
    s = scores(0, r0 + t)
    s_d = jnp.where(mask, s[:, r0:], NEG_INF)
    s = jnp.concatenate([s[:, :r0], s_d], axis=1) if r0 > 0 else s_d
    m = jnp.max(s, axis=-1, keepdims=True)
    acc = _dot(jnp.exp2(s - m).astype(jnp.bfloat16), vaug_ref[0:r0 + t, :])
    return acc[:, :LANES] / acc[:, LANES:]


def _mla_kernel(q_ref, k_ref, v_ref, o_ref, vaug_ref):
    tq = MLA_TILE
    seq = q_ref.shape[1]
    for p in range(MLA_PAIRS):
        vaug_ref[p, :, :LANES] = v_ref[0, :, p * LANES:(p + 1) * LANES]
        vaug_ref[p, :, LANES:] = jnp.ones((seq, LANES), vaug_ref.dtype)
    mask = _causal_mask(tq, tq)
    lane = lax.broadcasted_iota(jnp.int32, (tq, 2 * MLA_V_DIM), 1)
    for i in _tile_order(seq // tq):
        r0 = i * tq
        for p in range(MLA_PAIRS):
            outs = []
            for hh in range(2):
                lo = (2 * p + hh) * HEAD_SLAB
                q = q_ref[0, r0:r0 + tq, lo:lo + HEAD_SLAB]
                scores = lambda a, b, q=q, lo=lo: _dot_nt(q, k_ref[0, a:b, lo:lo + HEAD_SLAB])
                outs.append(_tile_attention(scores, vaug_ref.at[p], r0, mask))
            o_ref[0, r0:r0 + tq, p * LANES:(p + 1) * LANES] = jnp.where(lane < MLA_V_DIM, outs[0],
                                                                        outs[1]).astype(o_ref.dtype)


def _mla_attention(q, k, v):
    b, s, _ = q.shape
    steps = MLA_HEADS // (2 * MLA_PAIRS)
    blk = lambda w: pl.BlockSpec((1, s, MLA_PAIRS * w), lambda bi, p: (bi, 0, p))
    return pl.pallas_call(
        _mla_kernel,
        grid=(b, steps),
        in_specs=[blk(2 * HEAD_SLAB), blk(2 * HEAD_SLAB), blk(2 * MLA_V_DIM)],
        out_specs=blk(2 * MLA_V_DIM),
        out_shape=jax.ShapeDtypeStruct((b, s, MLA_HEADS * MLA_V_DIM), jnp.bfloat16),
        scratch_shapes=[pltpu.VMEM((MLA_PAIRS, s, 2 * LANES), jnp.bfloat16)],
        compiler_params=pltpu.CompilerParams(dimension_semantics=("parallel", "parallel"),
                                             vmem_limit_bytes=VMEM_LIMIT),
        name="mla_attn",
    )(q, k, v)


def _diff_kernel(slopes_ref, ok_ref, q_ref, k_ref, v_ref, kpos_ref, pk_ref, lq1_ref, lk1_ref, lq2_ref,
                 lk2_ref, g_ref, o_ref, vaug_ref, kaug_ref):
    tq = DIFF_TILE
    seq = q_ref.shape[1]
    vaug_ref[:, :LANES] = v_ref[0]
    vaug_ref[:, LANES:] = jnp.ones((seq, LANES), vaug_ref.dtype)
    both = lambda a: jnp.concatenate([a, a], axis=0)
    mask = both(_causal_mask(tq, tq))
    slope = slopes_ref[pl.program_id(1)] * LOG2E
    lane = lax.broadcasted_iota(jnp.int32, (tq, LANES), 1)
    lam = (jnp.exp(jnp.sum(lq1_ref[...] * lk1_ref[...], axis=-1, keepdims=True))
           - jnp.exp(jnp.sum(lq2_ref[...] * lk2_ref[...], axis=-1, keepdims=True)) + LAMBDA_INIT)

    def tiles(fast):
        if fast:
            kaug_ref[:, :LANES] = k_ref[0]
            kaug_ref[:, LANES:] = kpos_ref[0]
            coef = _piece_pattern(_bf16_pieces(jnp.full((1, 1), slope, jnp.float32)), True, 1, (1, LANES))
            coef = jnp.broadcast_to(coef.astype(jnp.bfloat16), (2 * tq, LANES))
        for i in _tile_order(seq // tq):
            r0 = i * tq
            q = q_ref[0, r0:r0 + tq, :]
            zero = jnp.zeros_like(q)
            qq = jnp.concatenate([jnp.where(lane < DIFF_HEAD_DIM, q, zero),
                                  jnp.where(lane < DIFF_HEAD_DIM, zero, q)], axis=0)
            if fast:
                qq = jnp.concatenate([qq, coef], axis=1)
                scores = lambda a, b, qq=qq: _dot_nt(qq, kaug_ref[a:b, :])
            else:
                pq = jnp.broadcast_to(pk_ref[0, :, r0:r0 + tq], (LANES, tq)).T[:, :1]

                def scores(a, b, qq=qq, pq=pq):
                    dist = jnp.abs(pq - pk_ref[0, :, a:b]).astype(jnp.float32)
                    return _dot_nt(qq, k_ref[0, a:b, :]) - both(dist * slope)

            out = _tile_attention(scores, vaug_ref, r0, mask)
            o = out[:tq] - lam * out[tq:]
            o_ref[0, r0:r0 + tq, :] = (_rms(o, g_ref[...]) * (1.0 - LAMBDA_INIT)).astype(o_ref.dtype)

    ok = ok_ref[pl.program_id(0)]
    pl.when(ok == 1)(functools.partial(tiles, True))
    pl.when(ok != 1)(functools.partial(tiles, False))


def _diff_attention(slopes, ok, q, k, v, kpos, pos_row, lq1, lk1, lq2, lk2, g):
    b, s, _ = q.shape
    vec = lambda a: pl.BlockSpec(a.shape, lambda bi, h: (0, 0))
    head = pl.BlockSpec((1, s, LANES), lambda bi, h: (bi, 0, h))
    smem = pl.BlockSpec(memory_space=pltpu.SMEM)
    return pl.pallas_call(
        _diff_kernel,
        grid=(b, DIFF_HEADS),
        in_specs=[smem, smem, head, head, head,
                  pl.BlockSpec((1, s, LANES), lambda bi, h: (bi, 0, 0)),
                  pl.BlockSpec((1, 1, s), lambda bi, h: (bi, 0, 0)),
                  vec(lq1), vec(lk1), vec(lq2), vec(lk2), vec(g)],
        out_specs=head,
        out_shape=jax.ShapeDtypeStruct((b, s, DIFF_HEADS * DIFF_V_DIM), jnp.bfloat16),
        scratch_shapes=[pltpu.VMEM((s, 2 * LANES), jnp.bfloat16)] * 2,
        compiler_params=pltpu.CompilerParams(dimension_semantics=("parallel", "parallel"),
                                             vmem_limit_bytes=VMEM_LIMIT),
        name="diff_attn",
    )(slopes, ok, q, k, v, kpos, pos_row, lq1, lk1, lq2, lk2, g)


def _post_kernel(x_ref, om_ref, od_ref, wg_ref, bg_ref, wom_ref, wod_ref, wo_ref, g1_ref, b1_ref,
                 wup_ref, wdn_ref, g2_ref, b2_ref, o_ref):
    def mix(rows):
        x = x_ref[rows, :]
        xb = x.astype(jnp.bfloat16)
        gate = jax.nn.sigmoid(_dot_nt(xb, wg_ref[...]) + bg_ref[...])
        y_mla = _dot(om_ref[rows, :], wom_ref[...])
        y_diff = _dot(od_ref[rows, :], wod_ref[...])
        merged = gate[:, :D_MODEL] * y_mla + gate[:, D_MODEL:] * y_diff
        mixed = _dot(merged.astype(jnp.bfloat16), wo_ref[...])
        return _layer_norm(DEEPNORM_ALPHA * x + mixed, g1_ref[...], b1_ref[...])

    def mlp(rows, x1):
        x1b = x1.astype(jnp.bfloat16)
        acc = DEEPNORM_ALPHA * x1
        for c in range(D_FF // FF_CHUNK):
            lo = c * FF_CHUNK
            hid = jnp.maximum(_dot(x1b, wup_ref[:, lo:lo + FF_CHUNK]), 0.0)
            acc = acc + _dot((hid * hid).astype(jnp.bfloat16), wdn_ref[lo:lo + FF_CHUNK, :])
        o_ref[rows, :] = _layer_norm(acc, g2_ref[...], b2_ref[...]).astype(o_ref.dtype)

    sub = x_ref.shape[0] // POST_GROUPS
    groups = [slice(g * sub, (g + 1) * sub) for g in range(POST_GROUPS)]
    x1s = [mix(rows) for rows in groups]
    for rows, x1 in zip(groups, x1s):
        mlp(rows, x1)


def _post(x2, o_mla, o_diff, w_in_t, bg, wom, wod, wo, g1, b1, wup, wdn, g2, b2):
    t, d = x2.shape
    rows = POST_ROWS
    tok = lambda w: pl.BlockSpec((rows, w), lambda i: (i, 0))
    consts = (bg, wom, wod, wo, g1, b1, wup, wdn, g2, b2)
    return pl.pallas_call(
        _post_kernel,
        grid=(t // rows,),
        in_specs=([tok(D_MODEL), tok(o_mla.shape[1]), tok(o_diff.shape[1]), _resident_rows(_O6, w_in_t.shape[0], d)]
                  + [_resident(a.shape) for a in consts]),
        out_specs=tok(D_MODEL),
        out_shape=jax.ShapeDtypeStruct((t, D_MODEL), jnp.float32),
        compiler_params=pltpu.CompilerParams(dimension_semantics=("parallel",), vmem_limit_bytes=VMEM_LIMIT),
        name="post",
    )(x2, o_mla, o_diff, w_in_t, *consts)


def _rot_half_cols(w):
    half = w.shape[-1] // 2
    return jnp.concatenate([-w[..., half:], w[..., :half]], axis=-1)


def _prep_weights(w_in, w_uq, w_ukv):
    bf = jnp.bfloat16
    f32 = jnp.float32
    pad_r = HEAD_SLAB - MLA_NOPE_DIM - MLA_ROPE_DIM
    d = w_in.shape[0]
    w_in_t = jnp.swapaxes(w_in, 0, 1).astype(bf)
    wkpe = w_in[:, _O2:_O3]
    slab = lambda w: jnp.concatenate([jnp.zeros((d, MLA_NOPE_DIM), f32), w, jnp.zeros((d, pad_r), f32)], axis=1)
    wkpe_t = jnp.concatenate([slab(wkpe), slab(_rot_half_cols(wkpe))], axis=1).T.astype(bf)

    r = w_uq.shape[0]
    uq = w_uq.reshape(r, MLA_HEADS, MLA_NOPE_DIM + MLA_ROPE_DIM)
    nope, pe = uq[..., :MLA_NOPE_DIM], uq[..., MLA_NOPE_DIM:]
    plain = jnp.concatenate([nope, pe, jnp.zeros((r, MLA_HEADS, pad_r), f32)], axis=-1)
    wuq = jnp.concatenate([plain.reshape(r, -1), _rot_half_cols(pe).reshape(r, -1)], axis=1).astype(bf)

    r = w_ukv.shape[0]
    ukv = w_ukv.reshape(r, MLA_HEADS, MLA_NOPE_DIM + MLA_V_DIM)
    kn, vv = ukv[..., :MLA_NOPE_DIM], ukv[..., MLA_NOPE_DIM:]
    kslab = jnp.concatenate([kn, jnp.zeros((r, MLA_HEADS, HEAD_SLAB - MLA_NOPE_DIM), f32)], axis=-1)
    wukv = jnp.concatenate([kslab.reshape(r, -1), vv.reshape(r, -1)], axis=1).astype(bf)
    return w_in_t, wkpe_t, wuq, wukv


def kernel(x, positions, w_in, b_gate, mla_q_norm, mla_kv_norm, w_uq, w_ukv, w_o_mla, diff_lambda_q1, diff_lambda_k1, diff_lambda_q2, diff_lambda_k2, diff_subln, w_o_diff, w_o, ln1_g, ln1_b, w_up, w_down, ln2_g, ln2_b):
    b, s, d = x.shape
    bf = jnp.bfloat16
    layer = 0
    w_in_t, wkpe_t, wuq, wukv = _prep_weights(w_in[layer], w_uq[layer], w_ukv[layer])

    inv_freq = 1.0 / (ROPE_THETA ** (jnp.arange(0, MLA_ROPE_DIM, 2, dtype=jnp.float32) / MLA_ROPE_DIM))
    x2 = x.reshape(b * s, d)
    rel = positions - positions[:, :1]
    pos_rows = jnp.concatenate([positions.reshape(-1, 1, PROJ_ROWS), rel.reshape(-1, 1, PROJ_ROWS)], axis=1)
    q, k, v, dq, dk, dv, kpos = _proj(x2, pos_rows, inv_freq[:, None],
                                      w_in_t, wkpe_t, wuq, wukv, mla_q_norm[layer][None, :],
                                      mla_kv_norm[layer][None, :])
    r3 = lambda a: a.reshape(b, s, a.shape[-1])
    o_mla = _mla_attention(r3(q), r3(k), r3(v))

    slopes = jnp.asarray(_alibi_slopes(DIFF_HEADS))
    ordered = jnp.all(positions[:, 1:] >= positions[:, :-1], axis=1)
    narrow = (jnp.min(rel, axis=1) >= 0) & (jnp.max(rel, axis=1) < ORDERED_SPAN_LIMIT)
    ok = (ordered & narrow).astype(jnp.int32)
    o_diff = _diff_attention(slopes, ok, r3(dq), r3(dk), r3(dv), r3(kpos), positions.reshape(b, 1, s),
                             diff_lambda_q1[layer][None, :], diff_lambda_k1[layer][None, :],
                             diff_lambda_q2[layer][None, :], diff_lambda_k2[layer][None, :],
                             diff_subln[layer][None, :])

    row = lambda a: a[layer][None, :]
    out = _post(x2, o_mla.reshape(b * s, -1), o_diff.reshape(b * s, -1), w_in_t, row(b_gate),
                w_o_mla[layer].astype(bf), w_o_diff[layer].astype(bf), w_o[layer].astype(bf),
                row(ln1_g), row(ln1_b), w_up[layer].astype(bf), w_down[layer].astype(bf), row(ln2_g), row(ln2_b))
    return out.reshape(b, s, d)
```

```python
import functools
import math

import jax
import jax.numpy as jnp
import numpy as np
from jax import lax
from jax.experimental import pallas as pl
from jax.experimental.pallas import tpu as pltpu

D_MODEL = 1024
MLA_HEADS = 8
MLA_NOPE_DIM = 64
MLA_ROPE_DIM = 32
MLA_V_DIM = 64
MLA_Q_RANK = 384
MLA_KV_RANK = 256
ROPE_THETA = 10000.0
DIFF_HEADS = 8
DIFF_HEAD_DIM = 64
DIFF_V_DIM = 128
D_FF = 4 * D_MODEL
DEPTH = 1
LN_EPS = 1e-5
RMS_EPS = 1e-6
NEG_INF = -1e30
DEEPNORM_ALPHA = (2.0 * DEPTH) ** 0.25
LAMBDA_INIT = 0.8 - 0.6 * math.exp(-0.3 * 0)
LOG2E = math.log2(math.e)

LANES = 128
HEAD_SLAB = LANES
VMEM_LIMIT = 56 * 1024 * 1024
PIECES = 3
ORDERED_SPAN_LIMIT = 1 << 14

PROJ_ROWS = 1024
MLA_TILE = 512
MLA_PAIRS = 4
DIFF_TILE = 256
POST_ROWS = 512
POST_GROUPS = 2
FF_CHUNK = 1024

_O1 = MLA_Q_RANK
_O2 = _O1 + MLA_KV_RANK
_O3 = _O2 + MLA_ROPE_DIM
_O4 = _O3 + DIFF_HEADS * 2 * DIFF_HEAD_DIM
_O5 = _O4 + DIFF_HEADS * 2 * DIFF_HEAD_DIM
_O6 = _O5 + DIFF_HEADS * DIFF_V_DIM


def _alibi_slopes(n):
    def pow2_slopes(k):
        start = 2.0 ** (-8.0 / k)
        return [start ** (i + 1) for i in range(k)]
    if math.log2(n).is_integer():
        s = pow2_slopes(n)
    else:
        c = 2 ** int(math.floor(math.log2(n)))
        s = pow2_slopes(c) + pow2_slopes(2 * c)[0::2][: n - c]
    return np.asarray(s, dtype=np.float32)


def _resident(shape):
    return pl.BlockSpec(shape, lambda *_: (0,) * len(shape), pipeline_mode=pl.Buffered(1))


def _resident_rows(start, stop, width):
    return pl.BlockSpec((pl.Element(stop - start), pl.Element(width)), lambda *_: (start, 0),
                        pipeline_mode=pl.Buffered(1))


def _rms(x, g):
    ms = jnp.mean(x * x, axis=-1, keepdims=True)
    return x * lax.rsqrt(ms + RMS_EPS) * g


def _layer_norm(x, g, b):
    mu = jnp.mean(x, axis=-1, keepdims=True)
    xc = x - mu
    var = jnp.mean(xc * xc, axis=-1, keepdims=True)
    return xc * lax.rsqrt(var + LN_EPS) * g + b


def _dot(a, b):
    return jnp.dot(a, b, preferred_element_type=jnp.float32)


def _dot_nt(a, b):
    return lax.dot_general(a, b, (((1,), (1,)), ((), ())), preferred_element_type=jnp.float32)


def _bf16_pieces(x):
    out = []
    for _ in range(PIECES):
        p = x.astype(jnp.bfloat16).astype(jnp.float32)
        out.append(p)
        x = x - p
    return out


def _piece_pattern(pieces, by_group, axis, shape):
    idx = lax.broadcasted_iota(jnp.int32, shape, axis)
    out = jnp.zeros(shape, jnp.float32)
    for a in range(PIECES):
        for b in range(PIECES):
            out = jnp.where(idx == PIECES * a + b, pieces[a if by_group else b], out)
    return out


def _token_tables(pos_row, rel_row, invf_col, scale):
    rows = pos_row.shape[1]
    half_r = MLA_ROPE_DIM // 2
    ang = invf_col * pos_row
    cos, sin = jnp.cos(ang), jnp.sin(ang)
    pad = jnp.zeros((HEAD_SLAB - MLA_NOPE_DIM - MLA_ROPE_DIM, rows), jnp.float32)
    nope0 = jnp.zeros((MLA_NOPE_DIM, rows), jnp.float32)
    slab = lambda nope, t: jnp.concatenate([nope, t, t, pad], axis=0).T
    pieces = _piece_pattern(_bf16_pieces(rel_row), False, 0, (half_r, rows))
    kpos = jnp.concatenate([pieces, jnp.zeros((LANES - half_r, rows), jnp.float32)], axis=0).T
    return (slab(nope0 + scale, cos * scale), slab(nope0, sin * scale), slab(nope0, cos), slab(nope0, sin), kpos)


def _proj_kernel(x_ref, pos_ref, invf_ref, wlat_ref, wkpe_ref, wdq_ref, wdk_ref, wdv_ref, wuq_ref, wukv_ref,
                 gq_ref, gkv_ref, q_ref, k_ref, v_ref, dq_ref, dk_ref, dv_ref, kpos_ref):
    xb = x_ref[...].astype(jnp.bfloat16)
    scale = (MLA_NOPE_DIM + MLA_ROPE_DIM) ** -0.5 * LOG2E
    pos = pos_ref[0].astype(jnp.float32)
    cos_s, sin_s, cos, sin, kpos = _token_tables(pos[0:1], pos[1:2], invf_ref[...], scale)
    kpos_ref[...] = kpos.astype(kpos_ref.dtype)

    lat = _dot_nt(xb, wlat_ref[...])

    cq = _rms(lat[:, :_O1], gq_ref[...]).astype(jnp.bfloat16)
    qq = _dot(cq, wuq_ref[...])
    half = MLA_HEADS * HEAD_SLAB
    per_group = LANES // MLA_ROPE_DIM
    for h in range(MLA_HEADS):
        lo = h * HEAD_SLAB
        grp = half + (h // per_group) * LANES
        rot = pltpu.roll(qq[:, grp:grp + LANES], (MLA_NOPE_DIM - (h % per_group) * MLA_ROPE_DIM) % LANES, 1)
        q_ref[:, lo:lo + HEAD_SLAB] = (qq[:, lo:lo + HEAD_SLAB] * cos_s + rot * sin_s).astype(q_ref.dtype)

    kp = _dot_nt(xb, wkpe_ref[...])
    kpe = kp[:, :HEAD_SLAB] * cos + kp[:, HEAD_SLAB:] * sin

    ckv = _rms(lat[:, _O1:_O2], gkv_ref[...]).astype(jnp.bfloat16)
    kv = _dot(ckv, wukv_ref[...])
    for h in range(MLA_HEADS):
        lo = h * HEAD_SLAB
        k_ref[:, lo:lo + HEAD_SLAB] = (kv[:, lo:lo + HEAD_SLAB] + kpe).astype(k_ref.dtype)
    v_ref[...] = kv[:, half:].astype(v_ref.dtype)

    dq_ref[...] = (_dot_nt(xb, wdq_ref[...]) * (DIFF_HEAD_DIM ** -0.5 * LOG2E)).astype(dq_ref.dtype)
    dk_ref[...] = _dot_nt(xb, wdk_ref[...]).astype(dk_ref.dtype)
    dv_ref[...] = _dot_nt(xb, wdv_ref[...]).astype(dv_ref.dtype)


def _proj(x2, pos_rows, invf, w_in_t, wkpe_t, wuq, wukv, gq, gkv):
    t, d = x2.shape
    rows = PROJ_ROWS
    bf = jnp.bfloat16
    tok = lambda w: pl.BlockSpec((rows, w), lambda i: (i, 0))
    wrows = lambda a, b: _resident_rows(a, b, d)
    out_shape = [jax.ShapeDtypeStruct((t, MLA_HEADS * HEAD_SLAB), bf),
                 jax.ShapeDtypeStruct((t, MLA_HEADS * HEAD_SLAB), bf),
                 jax.ShapeDtypeStruct((t, MLA_HEADS * MLA_V_DIM), bf),
                 jax.ShapeDtypeStruct((t, D_MODEL), bf),
                 jax.ShapeDtypeStruct((t, D_MODEL), bf),
                 jax.ShapeDtypeStruct((t, D_MODEL), bf),
                 jax.ShapeDtypeStruct((t, LANES), bf)]
    return pl.pallas_call(
        _proj_kernel,
        grid=(t // rows,),
        in_specs=[tok(D_MODEL), pl.BlockSpec((1, 2, rows), lambda i: (i, 0, 0)), _resident(invf.shape),
                  wrows(0, _O2), _resident(wkpe_t.shape), wrows(_O3, _O4), wrows(_O4, _O5), wrows(_O5, _O6),
                  _resident(wuq.shape), _resident(wukv.shape), _resident(gq.shape), _resident(gkv.shape)],
        out_specs=[tok(s.shape[1]) for s in out_shape],
        out_shape=out_shape,
        compiler_params=pltpu.CompilerParams(dimension_semantics=("parallel",), vmem_limit_bytes=VMEM_LIMIT),
        name="proj",
    )(x2, pos_rows, invf, w_in_t, wkpe_t, w_in_t, w_in_t, w_in_t, wuq, wukv, gq, gkv)


def _tile_order(n):
    return list(range(n - 1, -1, -1))


def _causal_mask(tq, tk):
    row = lax.broadcasted_iota(jnp.int32, (tq, tk), 0)
    col = lax.broadcasted_iota(jnp.int32, (tq, tk), 1)
    return col <= row


def _tile_attention(scores, vaug_ref, r0, mask):
    t = mask.shape[1]
    s = scores(0, r0 + t)
    s_d = jnp.where(mask, s[:, r0:], NEG_INF)
    s = jnp.concatenate([s[:, :r0], s_d], axis=1) if r0 > 0 else s_d
    m = jnp.max(s, axis=-1, keepdims=True)
    acc = _dot(jnp.exp2(s - m).astype(jnp.bfloat16), vaug_ref[0:r0 + t, :])
    return acc[:, :LANES] / acc[:, LANES:]


def _mla_kernel(q_ref, k_ref, v_ref, o_ref, vaug_ref):
    tq = MLA_TILE
    seq = q_ref.shape[1]
    for p in range(MLA_PAIRS):
        vaug_ref[p, :, :LANES] = v_ref[0, :, p * LANES:(p + 1) * LANES]
        vaug_ref[p, :, LANES:] = jnp.ones((seq, LANES), vaug_ref.dtype)
    mask = _causal_mask(tq, tq)
    lane = lax.broadcasted_iota(jnp.int32, (tq, 2 * MLA_V_DIM), 1)
    for i in _tile_order(seq // tq):
        r0 = i * tq
        for p in range(MLA_PAIRS):
            outs = []
            for hh in range(2):
                lo = (2 * p + hh) * HEAD_SLAB
                q = q_ref[0, r0:r0 + tq, lo:lo + HEAD_SLAB]
                scores = lambda a, b, q=q, lo=lo: _dot_nt(q, k_ref[0, a:b, lo:lo + HEAD_SLAB])
                outs.append(_tile_attention(scores, vaug_ref.at[p], r0, mask))
            o_ref[0, r0:r0 + tq, p * LANES:(p + 1) * LANES] = jnp.where(lane < MLA_V_DIM, outs[0],
                                                                        outs[1]).astype(o_ref.dtype)


def _mla_attention(q, k, v):
    b, s, _ = q.shape
    steps = MLA_HEADS // (2 * MLA_PAIRS)
    blk = lambda w: pl.BlockSpec((1, s, MLA_PAIRS * w), lambda bi, p: (bi, 0, p))
    return pl.pallas_call(
        _mla_kernel,
        grid=(b, steps),
        in_specs=[blk(2 * HEAD_SLAB), blk(2 * HEAD_SLAB), blk(2 * MLA_V_DIM)],
        out_specs=blk(2 * MLA_V_DIM),
        out_shape=jax.ShapeDtypeStruct((b, s, MLA_HEADS * MLA_V_DIM), jnp.bfloat16),
        scratch_shapes=[pltpu.VMEM((MLA_PAIRS, s, 2 * LANES), jnp.bfloat16)],
        compiler_params=pltpu.CompilerParams(dimension_semantics=("parallel", "parallel"),
                                             vmem_limit_bytes=VMEM_LIMIT),
        name="mla_attn",
    )(q, k, v)


def _diff_kernel(slopes_ref, ok_ref, q_ref, k_ref, v_ref, kpos_ref, pk_ref, lq1_ref, lk1_ref, lq2_ref,
                 lk2_ref, g_ref, o_ref, vaug_ref, kaug_ref):
    tq = DIFF_TILE
    seq = q_ref.shape[1]
    vaug_ref[:, :LANES] = v_ref[0]
    vaug_ref[:, LANES:] = jnp.ones((seq, LANES), vaug_ref.dtype)
    both = lambda a: jnp.concatenate([a, a], axis=0)
    mask = both(_causal_mask(tq, tq))
    slope = slopes_ref[pl.program_id(1)] * LOG2E
    lane = lax.broadcasted_iota(jnp.int32, (tq, LANES), 1)
    lam = (jnp.exp(jnp.sum(lq1_ref[...] * lk1_ref[...], axis=-1, keepdims=True))
           - jnp.exp(jnp.sum(lq2_ref[...] * lk2_ref[...], axis=-1, keepdims=True)) + LAMBDA_INIT)

    def tiles(fast):
        if fast:
            kaug_ref[:, :LANES] = k_ref[0]
            kaug_ref[:, LANES:] = kpos_ref[0]
            coef = _piece_pattern(_bf16_pieces(jnp.full((1, 1), slope, jnp.float32)), True, 1, (1, LANES))
            coef = jnp.broadcast_to(coef.astype(jnp.bfloat16), (2 * tq, LANES))
        for i in _tile_order(seq // tq):
            r0 = i * tq
            q = q_ref[0, r0:r0 + tq, :]
            zero = jnp.zeros_like(q)
            qq = jnp.concatenate([jnp.where(lane < DIFF_HEAD_DIM, q, zero),
                                  jnp.where(lane < DIFF_HEAD_DIM, zero, q)], axis=0)
            if fast:
                qq = jnp.concatenate([qq, coef], axis=1)
                scores = lambda a, b, qq=qq: _dot_nt(qq, kaug_ref[a:b, :])
            else:
                pq = jnp.broadcast_to(pk_ref[0, :, r0:r0 + tq], (LANES, tq)).T[:, :1]

                def scores(a, b, qq=qq, pq=pq):
                    dist = jnp.abs(pq - pk_ref[0, :, a:b]).astype(jnp.float32)
                    return _dot_nt(qq, k_ref[0, a:b, :]) - both(dist * slope)

            out = _tile_attention(scores, vaug_ref, r0, mask)
            o = out[:tq] - lam * out[tq:]
            o_ref[0, r0:r0 + tq, :] = (_rms(o, g_ref[...]) * (1.0 - LAMBDA_INIT)).astype(o_ref.dtype)

    ok = ok_ref[pl.program_id(0)]
    pl.when(ok == 1)(functools.partial(tiles, True))
    pl.when(ok != 1)(functools.partial(tiles, False))


def _diff_attention(slopes, ok, q, k, v, kpos, pos_row, lq1, lk1, lq2, lk2, g):
    b, s, _ = q.shape
    vec = lambda a: pl.BlockSpec(a.shape, lambda bi, h: (0, 0))
    head = pl.BlockSpec((1, s, LANES), lambda bi, h: (bi, 0, h))
    smem = pl.BlockSpec(memory_space=pltpu.SMEM)
    return pl.pallas_call(
        _diff_kernel,
        grid=(b, DIFF_HEADS),
        in_specs=[smem, smem, head, head, head,
                  pl.BlockSpec((1, s, LANES), lambda bi, h: (bi, 0, 0)),
                  pl.BlockSpec((1, 1, s), lambda bi, h: (bi, 0, 0)),
                  vec(lq1), vec(lk1), vec(lq2), vec(lk2), vec(g)],
        out_specs=head,
        out_shape=jax.ShapeDtypeStruct((b, s, DIFF_HEADS * DIFF_V_DIM), jnp.bfloat16),
        scratch_shapes=[pltpu.VMEM((s, 2 * LANES), jnp.bfloat16)] * 2,
        compiler_params=pltpu.CompilerParams(dimension_semantics=("parallel", "parallel"),
                                             vmem_limit_bytes=VMEM_LIMIT),
        name="diff_attn",
    )(slopes, ok, q, k, v, kpos, pos_row, lq1, lk1, lq2, lk2, g)


def _post_kernel(x_ref, om_ref, od_ref, wg_ref, bg_ref, wom_ref, wod_ref, wo_ref, g1_ref, b1_ref,
                 wup_ref, wdn_ref, g2_ref, b2_ref, o_ref):
    def mix(rows):
        x = x_ref[rows, :]
        xb = x.astype(jnp.bfloat16)
        gate = jax.nn.sigmoid(_dot_nt(xb, wg_ref[...]) + bg_ref[...])
        y_mla = _dot(om_ref[rows, :], wom_ref[...])
        y_diff = _dot(od_ref[rows, :], wod_ref[...])
        merged = gate[:, :D_MODEL] * y_mla + gate[:, D_MODEL:] * y_diff
        mixed = _dot(merged.astype(jnp.bfloat16), wo_ref[...])
        return _layer_norm(DEEPNORM_ALPHA * x + mixed, g1_ref[...], b1_ref[...])

    def mlp(rows, x1):
        x1b = x1.astype(jnp.bfloat16)
        acc = DEEPNORM_ALPHA * x1
        for c in range(D_FF // FF_CHUNK):
            lo = c * FF_CHUNK
            hid = jnp.maximum(_dot(x1b, wup_ref[:, lo:lo + FF_CHUNK]), 0.0)
            acc = acc + _dot((hid * hid).astype(jnp.bfloat16), wdn_ref[lo:lo + FF_CHUNK, :])
        o_ref[rows, :] = _layer_norm(acc, g2_ref[...], b2_ref[...]).astype(o_ref.dtype)

    sub = x_ref.shape[0] // POST_GROUPS
    groups = [slice(g * sub, (g + 1) * sub) for g in range(POST_GROUPS)]
    x1s = [mix(rows) for rows in groups]
    for rows, x1 in zip(groups, x1s):
        mlp(rows, x1)


def _post(x2, o_mla, o_diff, w_in_t, bg, wom, wod, wo, g1, b1, wup, wdn, g2, b2):
    t, d = x2.shape
    rows = POST_ROWS
    tok = lambda w: pl.BlockSpec((rows, w), lambda i: (i, 0))
    consts = (bg, wom, wod, wo, g1, b1, wup, wdn, g2, b2)
    return pl.pallas_call(
        _post_kernel,
        grid=(t // rows,),
        in_specs=([tok(D_MODEL), tok(o_mla.shape[1]), tok(o_diff.shape[1]), _resident_rows(_O6, w_in_t.shape[0], d)]
                  + [_resident(a.shape) for a in consts]),
        out_specs=tok(D_MODEL),
        out_shape=jax.ShapeDtypeStruct((t, D_MODEL), jnp.float32),
        compiler_params=pltpu.CompilerParams(dimension_semantics=("parallel",), vmem_limit_bytes=VMEM_LIMIT),
        name="post",
    )(x2, o_mla, o_diff, w_in_t, *consts)


def _rot_half_cols(w):
    half = w.shape[-1] // 2
    return jnp.concatenate([-w[..., half:], w[..., :half]], axis=-1)


def _prep_weights(w_in, w_uq, w_ukv):
    bf = jnp.bfloat16
    f32 = jnp.float32
    pad_r = HEAD_SLAB - MLA_NOPE_DIM - MLA_ROPE_DIM
    d = w_in.shape[0]
    w_in_t = jnp.swapaxes(w_in, 0, 1).astype(bf)
    wkpe = w_in[:, _O2:_O3]
    slab = lambda w: jnp.concatenate([jnp.zeros((d, MLA_NOPE_DIM), f32), w, jnp.zeros((d, pad_r), f32)], axis=1)
    wkpe_t = jnp.concatenate([slab(wkpe), slab(_rot_half_cols(wkpe))], axis=1).T.astype(bf)

    r = w_uq.shape[0]
    uq = w_uq.reshape(r, MLA_HEADS, MLA_NOPE_DIM + MLA_ROPE_DIM)
    nope, pe = uq[..., :MLA_NOPE_DIM], uq[..., MLA_NOPE_DIM:]
    plain = jnp.concatenate([nope, pe, jnp.zeros((r, MLA_HEADS, pad_r), f32)], axis=-1)
    wuq = jnp.concatenate([plain.reshape(r, -1), _rot_half_cols(pe).reshape(r, -1)], axis=1).astype(bf)

    r = w_ukv.shape[0]
    ukv = w_ukv.reshape(r, MLA_HEADS, MLA_NOPE_DIM + MLA_V_DIM)
    kn, vv = ukv[..., :MLA_NOPE_DIM], ukv[..., MLA_NOPE_DIM:]
    kslab = jnp.concatenate([kn, jnp.zeros((r, MLA_HEADS, HEAD_SLAB - MLA_NOPE_DIM), f32)], axis=-1)
    wukv = jnp.concatenate([kslab.reshape(r, -1), vv.reshape(r, -1)], axis=1).astype(bf)
    return w_in_t, wkpe_t, wuq, wukv


def kernel(x, positions, w_in, b_gate, mla_q_norm, mla_kv_norm, w_uq, w_ukv, w_o_mla, diff_lambda_q1, diff_lambda_k1, diff_lambda_q2, diff_lambda_k2, diff_subln, w_o_diff, w_o, ln1_g, ln1_b, w_up, w_down, ln2_g, ln2_b):
    b, s, d = x.shape
    bf = jnp.bfloat16
    layer = 0
    w_in_t, wkpe_t, wuq, wukv = _prep_weights(w_in[layer], w_uq[layer], w_ukv[layer])

    inv_freq = 1.0 / (ROPE_THETA ** (jnp.arange(0, MLA_ROPE_DIM, 2, dtype=jnp.float32) / MLA_ROPE_DIM))
    x2 = x.reshape(b * s, d)
    rel = positions - positions[:, :1]
    pos_rows = jnp.concatenate([positions.reshape(-1, 1, PROJ_ROWS), rel.reshape(-1, 1, PROJ_ROWS)], axis=1)
    q, k, v, dq, dk, dv, kpos = _proj(x2, pos_rows, inv_freq[:, None],
                                      w_in_t, wkpe_t, wuq, wukv, mla_q_norm[layer][None, :],
                                      mla_kv_norm[layer][None, :])
    r3 = lambda a: a.reshape(b, s, a.shape[-1])
    o_mla = _mla_attention(r3(q), r3(k), r3(v))

    slopes = jnp.asarray(_alibi_slopes(DIFF_HEADS))
    ordered = jnp.all(positions[:, 1:] >= positions[:, :-1], axis=1)
    narrow = (jnp.min(rel, axis=1) >= 0) & (jnp.max(rel, axis=1) < ORDERED_SPAN_LIMIT)
    ok = (ordered & narrow).astype(jnp.int32)
    o_diff = _diff_attention(slopes, ok, r3(dq), r3(dk), r3(dv), r3(kpos), positions.reshape(b, 1, s),
                             diff_lambda_q1[layer][None, :], diff_lambda_k1[layer][None, :],
                             diff_lambda_q2[layer][None, :], diff_lambda_k2[layer][None, :],
                             diff_subln[layer][None, :])

    row = lambda a: a[layer][None, :]
    out = _post(x2, o_mla.reshape(b * s, -1), o_diff.reshape(b * s, -1), w_in_t, row(b_gate),
                w_o_mla[layer].astype(bf), w_o_diff[layer].astype(bf), w_o[layer].astype(bf),
                row(ln1_g), row(ln1_b), w_up[layer].astype(bf), w_down[layer].astype(bf), row(ln2_g), row(ln2_b))
    return out.reshape(b, s, d)
```

```python
import functools
import math

import jax
import jax.numpy as jnp
import numpy as np
from jax import lax
from jax.experimental import pallas as pl
from jax.experimental.pallas import tpu as pltpu

D_MODEL = 1024
MLA_HEADS = 8
MLA_NOPE_DIM = 64
MLA_ROPE_DIM = 32
MLA_V_DIM = 64
MLA_Q_RANK = 384
MLA_KV_RANK = 256
ROPE_THETA = 10000.0
DIFF_HEADS = 8
DIFF_HEAD_DIM = 64
DIFF_V_DIM = 128
D_FF = 4 * D_MODEL
DEPTH = 1
LN_EPS = 1e-5
RMS_EPS = 1e-6
NEG_INF = -1e30
DEEPNORM_ALPHA = (2.0 * DEPTH) ** 0.25
LAMBDA_INIT = 0.8 - 0.6 * math.exp(-0.3 * 0)
LOG2E = math.log2(math.e)

LANES = 128
HEAD_SLAB = LANES
VMEM_LIMIT = 56 * 1024 * 1024
PIECES = 3
ORDERED_SPAN_LIMIT = 1 << 14

PROJ_ROWS = 1024
MLA_TILE = 512
MLA_PAIRS = 4
DIFF_TILE = 256
POST_ROWS = 512
POST_GROUPS = 2
FF_CHUNK = 1024

_O1 = MLA_Q_RANK
_O2 = _O1 + MLA_KV_RANK
_O3 = _O2 + MLA_ROPE_DIM
_O4 = _O3 + DIFF_HEADS * 2 * DIFF_HEAD_DIM
_O5 = _O4 + DIFF_HEADS * 2 * DIFF_HEAD_DIM
_O6 = _O5 + DIFF_HEADS * DIFF_V_DIM


def _alibi_slopes(n):
    def pow2_slopes(k):
        start = 2.0 ** (-8.0 / k)
        return [start ** (i + 1) for i in range(k)]
    if math.log2(n).is_integer():
        s = pow2_slopes(n)
    else:
        c = 2 ** int(math.floor(math.log2(n)))
        s = pow2_slopes(c) + pow2_slopes(2 * c)[0::2][: n - c]
    return np.asarray(s, dtype=np.float32)


def _resident(shape):
    return pl.BlockSpec(shape, lambda *_: (0,) * len(shape), pipeline_mode=pl.Buffered(1))


def _resident_rows(start, stop, width):
    return pl.BlockSpec((pl.Element(stop - start), pl.Element(width)), lambda *_: (start, 0),
                        pipeline_mode=pl.Buffered(1))


def _rms(x, g):
    ms = jnp.mean(x * x, axis=-1, keepdims=True)
    return x * lax.rsqrt(ms + RMS_EPS) * g


def _layer_norm(x, g, b):
    mu = jnp.mean(x, axis=-1, keepdims=True)
    xc = x - mu
    var = jnp.mean(xc * xc, axis=-1, keepdims=True)
    return xc * lax.rsqrt(var + LN_EPS) * g + b


def _dot(a, b):
    return jnp.dot(a, b, preferred_element_type=jnp.float32)


def _dot_nt(a, b):
    return lax.dot_general(a, b, (((1,), (1,)), ((), ())), preferred_element_type=jnp.float32)


def _bf16_pieces(x):
    out = []
    for _ in range(PIECES):
        p = x.astype(jnp.bfloat16).astype(jnp.float32)
        out.append(p)
        x = x - p
    return out


def _piece_pattern(pieces, by_group, axis, shape):
    idx = lax.broadcasted_iota(jnp.int32, shape, axis)
    out = jnp.zeros(shape, jnp.float32)
    for a in range(PIECES):
        for b in range(PIECES):
            out = jnp.where(idx == PIECES * a + b, pieces[a if by_group else b], out)
    return out


def _token_tables(pos_row, rel_row, invf_col, scale):
    rows = pos_row.shape[1]
    half_r = MLA_ROPE_DIM // 2
    ang = invf_col * pos_row
    cos, sin = jnp.cos(ang), jnp.sin(ang)
    pad = jnp.zeros((HEAD_SLAB - MLA_NOPE_DIM - MLA_ROPE_DIM, rows), jnp.float32)
    nope0 = jnp.zeros((MLA_NOPE_DIM, rows), jnp.float32)
    slab = lambda nope, t: jnp.concatenate([nope, t, t, pad], axis=0).T
    pieces = _piece_pattern(_bf16_pieces(rel_row), False, 0, (half_r, rows))
    kpos = jnp.concatenate([pieces, jnp.zeros((LANES - half_r, rows), jnp.float32)], axis=0).T
    return (slab(nope0 + scale, cos * scale), slab(nope0, sin * scale), slab(nope0, cos), slab(nope0, sin), kpos)


def _proj_kernel(x_ref, pos_ref, invf_ref, wlat_ref, wkpe_ref, wdq_ref, wdk_ref, wdv_ref, wuq_ref, wukv_ref,
                 gq_ref, gkv_ref, q_ref, k_ref, v_ref, dq_ref, dk_ref, dv_ref, kpos_ref):
    xb = x_ref[...].astype(jnp.bfloat16)
    scale = (MLA_NOPE_DIM + MLA_ROPE_DIM) ** -0.5 * LOG2E
    pos = pos_ref[0].astype(jnp.float32)
    cos_s, sin_s, cos, sin, kpos = _token_tables(pos[0:1], pos[1:2], invf_ref[...], scale)
    kpos_ref[...] = kpos.astype(kpos_ref.dtype)

    lat = _dot_nt(xb, wlat_ref[...])

    cq = _rms(lat[:, :_O1], gq_ref[...]).astype(jnp.bfloat16)
    qq = _dot(cq, wuq_ref[...])
    dk_ref[...] = _dot_nt(xb, wdk_ref[...]).astype(dk_ref.dtype)
    half = MLA_HEADS * HEAD_SLAB
    per_group = LANES // MLA_ROPE_DIM
    for h in range(MLA_HEADS):
        lo = h * HEAD_SLAB
        grp = half + (h // per_group) * LANES
        rot = pltpu.roll(qq[:, grp:grp + LANES], (MLA_NOPE_DIM - (h % per_group) * MLA_ROPE_DIM) % LANES, 1)
        q_ref[:, lo:lo + HEAD_SLAB] = (qq[:, lo:lo + HEAD_SLAB] * cos_s + rot * sin_s).astype(q_ref.dtype)

    kp = _dot_nt(xb, wkpe_ref[...])
    kpe = kp[:, :HEAD_SLAB] * cos + kp[:, HEAD_SLAB:] * sin

    ckv = _rms(lat[:, _O1:_O2], gkv_ref[...]).astype(jnp.bfloat16)
    kv = _dot(ckv, wukv_ref[...])
    dv_ref[...] = _dot_nt(xb, wdv_ref[...]).astype(dv_ref.dtype)
    for h in range(MLA_HEADS):
        lo = h * HEAD_SLAB
        k_ref[:, lo:lo + HEAD_SLAB] = (kv[:, lo:lo + HEAD_SLAB] + kpe).astype(k_ref.dtype)
    v_ref[...] = kv[:, half:].astype(v_ref.dtype)

    dq_ref[...] = (_dot_nt(xb, wdq_ref[...]) * (DIFF_HEAD_DIM ** -0.5 * LOG2E)).astype(dq_ref.dtype)


def _proj(x2, pos_rows, invf, w_in_t, wkpe_t, wuq, wukv, gq, gkv):
    t, d = x2.shape
    rows = PROJ_ROWS
    bf = jnp.bfloat16
    tok = lambda w: pl.BlockSpec((rows, w), lambda i: (i, 0))
    wrows = lambda a, b: _resident_rows(a, b, d)
    out_shape = [jax.ShapeDtypeStruct((t, MLA_HEADS * HEAD_SLAB), bf),
                 jax.ShapeDtypeStruct((t, MLA_HEADS * HEAD_SLAB), bf),
                 jax.ShapeDtypeStruct((t, MLA_HEADS * MLA_V_DIM), bf),
                 jax.ShapeDtypeStruct((t, D_MODEL), bf),
                 jax.ShapeDtypeStruct((t, D_MODEL), bf),
                 jax.ShapeDtypeStruct((t, D_MODEL), bf),
                 jax.ShapeDtypeStruct((t, LANES), bf)]
    return pl.pallas_call(
        _proj_kernel,
        grid=(t // rows,),
        in_specs=[tok(D_MODEL), pl.BlockSpec((1, 2, rows), lambda i: (i, 0, 0)), _resident(invf.shape),
                  wrows(0, _O2), _resident(wkpe_t.shape), wrows(_O3, _O4), wrows(_O4, _O5), wrows(_O5, _O6),
                  _resident(wuq.shape), _resident(wukv.shape), _resident(gq.shape), _resident(gkv.shape)],
        out_specs=[tok(s.shape[1]) for s in out_shape],
        out_shape=out_shape,
        compiler_params=pltpu.CompilerParams(dimension_semantics=("parallel",), vmem_limit_bytes=VMEM_LIMIT),
        name="proj",
    )(x2, pos_rows, invf, w_in_t, wkpe_t, w_in_t, w_in_t, w_in_t, wuq, wukv, gq, gkv)


def _tile_order(n):
    return list(range(n - 1, -1, -1))


def _causal_mask(tq, tk):
    row = lax.broadcasted_iota(jnp.int32, (tq, tk), 0)
    col = lax.broadcasted_iota(jnp.int32, (tq, tk), 1)
    return col <= row


def _tile_attention(scores, vaug_ref, r0, mask):
    t = mask.shape[1]
    s = scores(0, r0 + t)
    s_d = jnp.where(mask, s[:, r0:], NEG_INF)
    s = jnp.concatenate([s[:, :r0], s_d], axis=1) if r0 > 0 else s_d
    m = jnp.max(s, axis=-1, keepdims=True)
    acc = _dot(jnp.exp2(s - m).astype(jnp.bfloat16), vaug_ref[0:r0 + t, :])
    return acc[:, :LANES] / acc[:, LANES:]


def _mla_kernel(q_ref, k_ref, v_ref, o_ref, vaug_ref):
    tq = MLA_TILE
    seq = q_ref.shape[1]
    for p in range(MLA_PAIRS):
        vaug_ref[p, :, :LANES] = v_ref[0, :, p * LANES:(p + 1) * LANES]
        vaug_ref[p, :, LANES:] = jnp.ones((seq, LANES), vaug_ref.dtype)
    mask = _causal_mask(tq, tq)
    lane = lax.broadcasted_iota(jnp.int32, (tq, 2 * MLA_V_DIM), 1)
    for i in _tile_order(seq // tq):
        r0 = i * tq
        for p in range(MLA_PAIRS):
            outs = []
            for hh in range(2):
                lo = (2 * p + hh) * HEAD_SLAB
                q = q_ref[0, r0:r0 + tq, lo:lo + HEAD_SLAB]
                scores = lambda a, b, q=q, lo=lo: _dot_nt(q, k_ref[0, a:b, lo:lo + HEAD_SLAB])
                outs.append(_tile_attention(scores, vaug_ref.at[p], r0, mask))
            o_ref[0, r0:r0 + tq, p * LANES:(p + 1) * LANES] = jnp.where(lane < MLA_V_DIM, outs[0],
                                                                        outs[1]).astype(o_ref.dtype)


def _mla_attention(q, k, v):
    b, s, _ = q.shape
    steps = MLA_HEADS // (2 * MLA_PAIRS)
    blk = lambda w: pl.BlockSpec((1, s, MLA_PAIRS * w), lambda bi, p: (bi, 0, p))
    return pl.pallas_call(
        _mla_kernel,
        grid=(b, steps),
        in_specs=[blk(2 * HEAD_SLAB), blk(2 * HEAD_SLAB), blk(2 * MLA_V_DIM)],
        out_specs=blk(2 * MLA_V_DIM),
        out_shape=jax.ShapeDtypeStruct((b, s, MLA_HEADS * MLA_V_DIM), jnp.bfloat16),
        scratch_shapes=[pltpu.VMEM((MLA_PAIRS, s, 2 * LANES), jnp.bfloat16)],
        compiler_params=pltpu.CompilerParams(dimension_semantics=("parallel", "parallel"),
                                             vmem_limit_bytes=VMEM_LIMIT),
        name="mla_attn",
    )(q, k, v)


def _diff_kernel(slopes_ref, ok_ref, q_ref, k_ref, v_ref, kpos_ref, pk_ref, lq1_ref, lk1_ref, lq2_ref,
                 lk2_ref, g_ref, o_ref, vaug_ref, kaug_ref):
    tq = DIFF_TILE
    seq = q_ref.shape[1]
    vaug_ref[:, :LANES] = v_ref[0]
    vaug_ref[:, LANES:] = jnp.ones((seq, LANES), vaug_ref.dtype)
    both = lambda a: jnp.concatenate([a, a], axis=0)
    mask = both(_causal_mask(tq, tq))
    slope = slopes_ref[pl.program_id(1)] * LOG2E
    lane = lax.broadcasted_iota(jnp.int32, (tq, LANES), 1)
    lam = (jnp.exp(jnp.sum(lq1_ref[...] * lk1_ref[...], axis=-1, keepdims=True))
           - jnp.exp(jnp.sum(lq2_ref[...] * lk2_ref[...], axis=-1, keepdims=True)) + LAMBDA_INIT)

    def tiles(fast):
        if fast:
            kaug_ref[:, :LANES] = k_ref[0]
            kaug_ref[:, LANES:] = kpos_ref[0]
            coef = _piece_pattern(_bf16_pieces(jnp.full((1, 1), slope, jnp.float32)), True, 1, (1, LANES))
            coef = jnp.broadcast_to(coef.astype(jnp.bfloat16), (2 * tq, LANES))
        for i in _tile_order(seq // tq):
            r0 = i * tq
            q = q_ref[0, r0:r0 + tq, :]
            zero = jnp.zeros_like(q)
            qq = jnp.concatenate([jnp.where(lane < DIFF_HEAD_DIM, q, zero),
                                  jnp.where(lane < DIFF_HEAD_DIM, zero, q)], axis=0)
            if fast:
                qq = jnp.concatenate([qq, coef], axis=1)
                scores = lambda a, b, qq=qq: _dot_nt(qq, kaug_ref[a:b, :])
            else:
                pq = jnp.broadcast_to(pk_ref[0, :, r0:r0 + tq], (LANES, tq)).T[:, :1]

                def scores(a, b, qq=qq, pq=pq):
                    dist = jnp.abs(pq - pk_ref[0, :, a:b]).astype(jnp.float32)
                    return _dot_nt(qq, k_ref[0, a:b, :]) - both(dist * slope)

            out = _tile_attention(scores, vaug_ref, r0, mask)
            o = out[:tq] - lam * out[tq:]
            o_ref[0, r0:r0 + tq, :] = (_rms(o, g_ref[...]) * (1.0 - LAMBDA_INIT)).astype(o_ref.dtype)

    ok = ok_ref[pl.program_id(0)]
    pl.when(ok == 1)(functools.partial(tiles, True))
    pl.when(ok != 1)(functools.partial(tiles, False))


def _diff_attention(slopes, ok, q, k, v, kpos, pos_row, lq1, lk1, lq2, lk2, g):
    b, s, _ = q.shape
    vec = lambda a: pl.BlockSpec(a.shape, lambda bi, h: (0, 0))
    head = pl.BlockSpec((1, s, LANES), lambda bi, h: (bi, 0, h))
    smem = pl.BlockSpec(memory_space=pltpu.SMEM)
    return pl.pallas_call(
        _diff_kernel,
        grid=(b, DIFF_HEADS),
        in_specs=[smem, smem, head, head, head,
                  pl.BlockSpec((1, s, LANES), lambda bi, h: (bi, 0, 0)),
                  pl.BlockSpec((1, 1, s), lambda bi, h: (bi, 0, 0)),
                  vec(lq1), vec(lk1), vec(lq2), vec(lk2), vec(g)],
        out_specs=head,
        out_shape=jax.ShapeDtypeStruct((b, s, DIFF_HEADS * DIFF_V_DIM), jnp.bfloat16),
        scratch_shapes=[pltpu.VMEM((s, 2 * LANES), jnp.bfloat16)] * 2,
        compiler_params=pltpu.CompilerParams(dimension_semantics=("parallel", "parallel"),
                                             vmem_limit_bytes=VMEM_LIMIT),
        name="diff_attn",
    )(slopes, ok, q, k, v, kpos, pos_row, lq1, lk1, lq2, lk2, g)


def _post_kernel(x_ref, om_ref, od_ref, wg_ref, bg_ref, wom_ref, wod_ref, wo_ref, g1_ref, b1_ref,
                 wup_ref, wdn_ref, g2_ref, b2_ref, o_ref):
    def mix(rows):
        x = x_ref[rows, :]
        xb = x.astype(jnp.bfloat16)
        gate = jax.nn.sigmoid(_dot_nt(xb, wg_ref[...]) + bg_ref[...])
        y_mla = _dot(om_ref[rows, :], wom_ref[...])
        y_diff = _dot(od_ref[rows, :], wod_ref[...])
        merged = gate[:, :D_MODEL] * y_mla + gate[:, D_MODEL:] * y_diff
        mixed = _dot(merged.astype(jnp.bfloat16), wo_ref[...])
        return _layer_norm(DEEPNORM_ALPHA * x + mixed, g1_ref[...], b1_ref[...])

    def mlp(rows, x1):
        x1b = x1.astype(jnp.bfloat16)
        acc = DEEPNORM_ALPHA * x1
        for c in range(D_FF // FF_CHUNK):
            lo = c * FF_CHUNK
            hid = jnp.maximum(_dot(x1b, wup_ref[:, lo:lo + FF_CHUNK]), 0.0)
            acc = acc + _dot((hid * hid).astype(jnp.bfloat16), wdn_ref[lo:lo + FF_CHUNK, :])
        o_ref[rows, :] = _layer_norm(acc, g2_ref[...], b2_ref[...]).astype(o_ref.dtype)

    sub = x_ref.shape[0] // POST_GROUPS
    groups = [slice(g * sub, (g + 1) * sub) for g in range(POST_GROUPS)]
    x1s = [mix(rows) for rows in groups]
    for rows, x1 in zip(groups, x1s):
        mlp(rows, x1)


def _post(x2, o_mla, o_diff, w_in_t, bg, wom, wod, wo, g1, b1, wup, wdn, g2, b2):
    t, d = x2.shape
    rows = POST_ROWS
    tok = lambda w: pl.BlockSpec((rows, w), lambda i: (i, 0))
    consts = (bg, wom, wod, wo, g1, b1, wup, wdn, g2, b2)
    return pl.pallas_call(
        _post_kernel,
        grid=(t // rows,),
        in_specs=([tok(D_MODEL), tok(o_mla.shape[1]), tok(o_diff.shape[1]), _resident_rows(_O6, w_in_t.shape[0], d)]
                  + [_resident(a.shape) for a in consts]),
        out_specs=tok(D_MODEL),
        out_shape=jax.ShapeDtypeStruct((t, D_MODEL), jnp.float32),
        compiler_params=pltpu.CompilerParams(dimension_semantics=("parallel",), vmem_limit_bytes=VMEM_LIMIT),
        name="post",
    )(x2, o_mla, o_diff, w_in_t, *consts)


def _rot_half_cols(w):
    half = w.shape[-1] // 2
    return jnp.concatenate([-w[..., half:], w[..., :half]], axis=-1)


def _prep_weights(w_in, w_uq, w_ukv):
    bf = jnp.bfloat16
    f32 = jnp.float32
    pad_r = HEAD_SLAB - MLA_NOPE_DIM - MLA_ROPE_DIM
    d = w_in.shape[0]
    w_in_t = jnp.swapaxes(w_in, 0, 1).astype(bf)
    wkpe = w_in[:, _O2:_O3]
    slab = lambda w: jnp.concatenate([jnp.zeros((d, MLA_NOPE_DIM), f32), w, jnp.zeros((d, pad_r), f32)], axis=1)
    wkpe_t = jnp.concatenate([slab(wkpe), slab(_rot_half_cols(wkpe))], axis=1).T.astype(bf)

    r = w_uq.shape[0]
    uq = w_uq.reshape(r, MLA_HEADS, MLA_NOPE_DIM + MLA_ROPE_DIM)
    nope, pe = uq[..., :MLA_NOPE_DIM], uq[..., MLA_NOPE_DIM:]
    plain = jnp.concatenate([nope, pe, jnp.zeros((r, MLA_HEADS, pad_r), f32)], axis=-1)
    wuq = jnp.concatenate([plain.reshape(r, -1), _rot_half_cols(pe).reshape(r, -1)], axis=1).astype(bf)

    r = w_ukv.shape[0]
    ukv = w_ukv.reshape(r, MLA_HEADS, MLA_NOPE_DIM + MLA_V_DIM)
    kn, vv = ukv[..., :MLA_NOPE_DIM], ukv[..., MLA_NOPE_DIM:]
    kslab = jnp.concatenate([kn, jnp.zeros((r, MLA_HEADS, HEAD_SLAB - MLA_NOPE_DIM), f32)], axis=-1)
    wukv = jnp.concatenate([kslab.reshape(r, -1), vv.reshape(r, -1)], axis=1).astype(bf)
    return w_in_t, wkpe_t, wuq, wukv


def kernel(x, positions, w_in, b_gate, mla_q_norm, mla_kv_norm, w_uq, w_ukv, w_o_mla, diff_lambda_q1, diff_lambda_k1, diff_lambda_q2, diff_lambda_k2, diff_subln, w_o_diff, w_o, ln1_g, ln1_b, w_up, w_down, ln2_g, ln2_b):
    b, s, d = x.shape
    bf = jnp.bfloat16
    layer = 0
    w_in_t, wkpe_t, wuq, wukv = _prep_weights(w_in[layer], w_uq[layer], w_ukv[layer])

    inv_freq = 1.0 / (ROPE_THETA ** (jnp.arange(0, MLA_ROPE_DIM, 2, dtype=jnp.float32) / MLA_ROPE_DIM))
    x2 = x.reshape(b * s, d)
    rel = positions - positions[:, :1]
    pos_rows = jnp.concatenate([positions.reshape(-1, 1, PROJ_ROWS), rel.reshape(-1, 1, PROJ_ROWS)], axis=1)
    q, k, v, dq, dk, dv, kpos = _proj(x2, pos_rows, inv_freq[:, None],
                                      w_in_t, wkpe_t, wuq, wukv, mla_q_norm[layer][None, :],
                                      mla_kv_norm[layer][None, :])
    r3 = lambda a: a.reshape(b, s, a.shape[-1])
    o_mla = _mla_attention(r3(q), r3(k), r3(v))

    slopes = jnp.asarray(_alibi_slopes(DIFF_HEADS))
    ordered = jnp.all(positions[:, 1:] >= positions[:, :-1], axis=1)
    narrow = (jnp.min(rel, axis=1) >= 0) & (jnp.max(rel, axis=1) < ORDERED_SPAN_LIMIT)
    ok = (ordered & narrow).astype(jnp.int32)
    o_diff = _diff_attention(slopes, ok, r3(dq), r3(dk), r3(dv), r3(kpos), positions.reshape(b, 1, s),
                             diff_lambda_q1[layer][None, :], diff_lambda_k1[layer][None, :],
                             diff_lambda_q2[layer][None, :], diff_lambda_k2[layer][None, :],
                             diff_subln[layer][None, :])

    row = lambda a: a[layer][None, :]
    out = _post(x2, o_mla.reshape(b * s, -1), o_diff.reshape(b * s, -1), w_in_t, row(b_gate),
                w_o_mla[layer].astype(bf), w_o_diff[layer].astype(bf), w_o[layer].astype(bf),
                row(ln1_g), row(ln1_b), w_up[layer].astype(bf), w_down[layer].astype(bf), row(ln2_g), row(ln2_b))
    return out.reshape(b, s, d)
```
